```python
import jax, jax.numpy as jnp
from jax import lax
import numpy as np

D_MODEL = 1024
BATCH = 32
SEQ = 256
DEPTH = 4
DEC_BATCH = 4
DEC_SEQ = 1024
PAST_LEN = 256

GRID_W = 64
D_FF = 4 * D_MODEL
H_R = 4
DK_R = 128
DV_R = 128
W_R = H_R * DV_R
H_A = 8
KV_A = 2
HEAD_DIM = 64
W_A = H_A * HEAD_DIM
MIX_W = W_R + W_A
ROPE_AXIS_DIM = HEAD_DIM // 2
ROPE_THETA = 10000.0
CHUNK = 16
Q_BLOCK = 128
EPS = 1e-6
IN_SIZES = (H_R * DK_R, H_R * DK_R, H_R * DK_R, W_R, W_R, W_A, KV_A * HEAD_DIM, KV_A * HEAD_DIM)
IN_W = 3 * H_R * DK_R + 2 * W_R + W_A + 2 * KV_A * HEAD_DIM

kernel_name = "hymba_hgrn2_gqa_axial_dit_step"


def rmsnorm(x, g):
    xf = x.astype(jnp.float32)
    y = xf * lax.rsqrt(jnp.mean(xf * xf, axis=-1, keepdims=True) + EPS)
    return (y * g.astype(jnp.float32)).astype(x.dtype)


def axial_rope(T):
    rows = T // GRID_W
    row = jnp.repeat(jnp.arange(rows, dtype=jnp.float32), GRID_W)
    col = jnp.tile(jnp.arange(GRID_W, dtype=jnp.float32), rows)
    inv = ROPE_THETA ** (-jnp.arange(0, ROPE_AXIS_DIM, 2, dtype=jnp.float32) / ROPE_AXIS_DIM)
    ar = row[:, None] * inv[None, :]
    ac = col[:, None] * inv[None, :]
    ang = jnp.concatenate([ar, ar, ac, ac], axis=-1)
    return jnp.cos(ang), jnp.sin(ang)


def apply_rope(x, cos, sin):
    B, T, H, Dh = x.shape
    xa = x.reshape(B, T, H, 2, 2, ROPE_AXIS_DIM // 2)
    rot = jnp.stack([-xa[..., 1, :], xa[..., 0, :]], axis=-2).reshape(x.shape)
    c = cos.astype(x.dtype)[None, :, None, :]
    s = sin.astype(x.dtype)[None, :, None, :]
    return x * c + rot * s


def forget_log(fx, lb):
    lb = lb.reshape(H_R, DK_R)
    return jnp.logaddexp(jnp.log(lb), jnp.log1p(-lb) + jax.nn.log_sigmoid(fx.astype(jnp.float32)))


def gla_chunked(q, k, v, log_f, s0):
    B, T, H, DK = q.shape
    DV = v.shape[-1]
    n = T // CHUNK

    def rs(a):
        return a.reshape(B, n, CHUNK, H, a.shape[-1]).transpose(1, 0, 3, 2, 4)

    qc, kc, vc, gc = rs(q), rs(k), rs(v), rs(log_f)
    b = jnp.cumsum(gc, axis=3)
    mask = jnp.tril(jnp.ones((CHUNK, CHUNK), dtype=bool))
    diff = b[..., :, None, :] - b[..., None, :, :]
    decay = jnp.exp(jnp.where(mask[:, :, None], diff, -jnp.inf))
    a_intra = jnp.einsum('nbhtd,nbhsd,nbhtsd->nbhts', qc, kc, decay)
    o_intra = jnp.einsum('nbhts,nbhsv->nbhtv', a_intra, vc)
    q_in = qc * jnp.exp(b)
    b_last = b[..., -1:, :]
    chunk_kv = jnp.einsum('nbhsd,nbhsv->nbhdv', kc * jnp.exp(b_last - b), vc)
    chunk_decay = jnp.exp(b_last[..., 0, :])

    def step(S, inp):
        qi, dec, kv = inp
        o = jnp.einsum('bhtd,bhdv->bhtv', qi, S)
        return dec[..., None] * S + kv, o

    s_fin, o_inter = lax.scan(step, s0.astype(jnp.float32), (q_in, chunk_decay, chunk_kv))
    o = (o_intra + o_inter).transpose(1, 0, 3, 2, 4).reshape(B, T, H, DV)
    return o, s_fin


def attend_blocks(q, k, v):
    B, T, H, Dh = q.shape
    G = H // KV_A
    nb = T // Q_BLOCK
    qb = q.reshape(B, nb, Q_BLOCK, KV_A, G, Dh).transpose(1, 0, 2, 3, 4, 5)
    kf = k.astype(jnp.float32)
    scale = HEAD_DIM ** -0.5

    def one(qi):
        s = jnp.einsum('bqkgd,bskd->bkgqs', qi.astype(jnp.float32), kf) * scale
        p = jax.nn.softmax(s, axis=-1).astype(v.dtype)
        return jnp.einsum('bkgqs,bskd->bqkgd', p, v)

    o = lax.map(one, qb)
    return o.transpose(1, 0, 2, 3, 4, 5).reshape(B, T, H * Dh)


def trunk_layer(x, mod, p, lb_f, lb_b, rope, ctx_k, ctx_v, s0_f, s0_b):
    B, T, _ = x.shape
    sh1, sc1, g1, sh2, sc2, g2 = jnp.split(mod, 6, axis=-1)
    h = rmsnorm(x, p['norm1_g']) * (1 + sc1) + sh1
    proj = h @ p['w_in']
    q_r, f_f, f_b, i_r, g_r, q_a, k_a, v_a = jnp.split(proj, list(np.cumsum(IN_SIZES)[:-1]), axis=-1)

    qr = q_r.reshape(B, T, H_R, DK_R).astype(jnp.float32)
    vr = i_r.reshape(B, T, H_R, DV_R).astype(jnp.float32)
    lf_f = forget_log(f_f.reshape(B, T, H_R, DK_R), lb_f)
    lf_b = forget_log(f_b.reshape(B, T, H_R, DK_R), lb_b)
    o_f, s_f = gla_chunked(qr, -jnp.expm1(lf_f), vr, lf_f, s0_f)
    o_b, s_b = gla_chunked(qr[:, ::-1], -jnp.expm1(lf_b)[:, ::-1], vr[:, ::-1], lf_b[:, ::-1], s0_b)
    o_r = o_f + o_b[:, ::-1]
    o_r = rmsnorm(o_r, p['hgrn_norm_g']).reshape(B, T, W_R).astype(x.dtype) * jax.nn.silu(g_r)

    qa = rmsnorm(q_a.reshape(B, T, H_A, HEAD_DIM), p['q_norm_g'])
    ka = rmsnorm(k_a.reshape(B, T, KV_A, HEAD_DIM), p['k_norm_g'])
    va = v_a.reshape(B, T, KV_A, HEAD_DIM)
    if rope is None:
        k_all, v_all = ka, va
    else:
        cos, sin = rope
        qa = apply_rope(qa, cos, sin)
        ka = apply_rope(ka, cos, sin)
        k_all = jnp.concatenate([ctx_k.astype(ka.dtype), ka], axis=1)
        v_all = jnp.concatenate([ctx_v.astype(va.dtype), va], axis=1)
    o_a = attend_blocks(qa, k_all, v_all)

    x = x + g1 * (jnp.concatenate([o_r, o_a], axis=-1) @ p['w_out'])
    h2 = rmsnorm(x, p['norm2_g']) * (1 + sc2) + sh2
    x = x + g2 * (jnp.square(jax.nn.relu(h2 @ p['w1'])) @ p['w2'])
    return x, ka, va, s_f, s_b


def setup_inputs(seed: int = 0) -> dict:
    key = jax.random.key(seed)
    ks = jax.random.split(key, 20)
    f32 = jnp.float32
    nrm = lambda k, shape, s: jax.random.normal(k, shape, f32) * s
    return {
        'x_prompt': nrm(ks[0], (BATCH, SEQ, D_MODEL), 1.0),
        'x_sample': nrm(ks[1], (DEC_BATCH, DEC_SEQ, D_MODEL), 1.0),
        'cache_k': nrm(ks[2], (DEC_BATCH, DEPTH, PAST_LEN, KV_A, HEAD_DIM), 1.0),
        'cache_v': nrm(ks[3], (DEC_BATCH, DEPTH, PAST_LEN, KV_A, HEAD_DIM), 1.0),
        'state_hgrn': nrm(ks[4], (DEC_BATCH, DEPTH, 2, H_R, DK_R, DV_R), 1.0),
        'c': nrm(ks[5], (DEC_BATCH, D_MODEL), 1.0),
        'c_ctx': nrm(ks[6], (D_MODEL,), 1.0),
        'w_mod': nrm(ks[7], (DEPTH, D_MODEL, 6 * D_MODEL), 0.5 * D_MODEL ** -0.5),
        'b_mod': nrm(ks[8], (DEPTH, 6 * D_MODEL), 0.02),
        'norm1_g': 1.0 + nrm(ks[9], (DEPTH, D_MODEL), 0.02),
        'w_in': nrm(ks[10], (DEPTH, D_MODEL, IN_W), D_MODEL ** -0.5),
        'lb_raw': nrm(ks[11], (DEPTH, 2, H_R * DK_R), 1.0),
        'hgrn_norm_g': 1.0 + nrm(ks[12], (DEPTH, DV_R), 0.02),
        'q_norm_g': 1.0 + nrm(ks[13], (DEPTH, HEAD_DIM), 0.02),
        'k_norm_g': 1.0 + nrm(ks[14], (DEPTH, HEAD_DIM), 0.02),
        'w_out': nrm(ks[15], (DEPTH, MIX_W, D_MODEL), MIX_W ** -0.5),
        'norm2_g': 1.0 + nrm(ks[16], (DEPTH, D_MODEL), 0.02),
        'w1': nrm(ks[17], (DEPTH, D_MODEL, D_FF), D_MODEL ** -0.5),
        'w2': nrm(ks[18], (DEPTH, D_FF, D_MODEL), D_FF ** -0.5),
        'final_norm_g': 1.0 + nrm(ks[19], (D_MODEL,), 0.02),
    }


def reference(x_prompt, x_sample, cache_k, cache_v, state_hgrn, c, c_ctx, w_mod, b_mod, norm1_g, w_in, lb_raw,
              hgrn_norm_g, q_norm_g, k_norm_g, w_out, norm2_g, w1, w2, final_norm_g):
    lb_all = jnp.cumsum(jax.nn.softmax(lb_raw.astype(jnp.float32), axis=0), axis=0)
    lb_all = lb_all - lb_all[0:1]

    def params(l):
        return {'norm1_g': norm1_g[l], 'w_in': w_in[l], 'hgrn_norm_g': hgrn_norm_g[l], 'q_norm_g': q_norm_g[l],
                'k_norm_g': k_norm_g[l], 'w_out': w_out[l], 'norm2_g': norm2_g[l], 'w1': w1[l], 'w2': w2[l]}

    xp = x_prompt
    Bp = x_prompt.shape[0]
    zero_s = jnp.zeros((Bp, H_R, DK_R, DV_R), jnp.float32)
    ks_list, vs_list, ss_list = [], [], []
    for l in range(DEPTH):
        mod_ctx = (jax.nn.silu(c_ctx) @ w_mod[l] + b_mod[l])[None, None, :]
        xp, k_l, v_l, sf, sb = trunk_layer(xp, mod_ctx, params(l), lb_all[l, 0], lb_all[l, 1], None, None, None,
                                           zero_s, zero_s)
        ks_list.append(k_l)
        vs_list.append(v_l)
        ss_list.append(jnp.stack([sf, sb], axis=1))
    y_prompt = rmsnorm(xp, final_norm_g)
    new_k = jnp.stack(ks_list, axis=1)
    new_v = jnp.stack(vs_list, axis=1)
    new_s = jnp.stack(ss_list, axis=1)

    rope = axial_rope(x_sample.shape[1])
    xs = x_sample
    for l in range(DEPTH):
        mod_lat = (jax.nn.silu(c) @ w_mod[l] + b_mod[l])[:, None, :]
        xs, _, _, _, _ = trunk_layer(xs, mod_lat, params(l), lb_all[l, 0], lb_all[l, 1], rope,
                                     cache_k[:, l], cache_v[:, l], state_hgrn[:, l, 0], state_hgrn[:, l, 1])
    y_sample = rmsnorm(xs, final_norm_g)
    return (y_prompt, y_sample, new_k, new_v, new_s)
```

```python
import functools

import jax
import jax.numpy as jnp
import numpy as np
from jax import lax
from jax.experimental import pallas as pl
from jax.experimental.pallas import tpu as pltpu

F32 = jnp.float32
BF16 = jnp.bfloat16

D_MODEL = 1024
DEPTH = 4
GRID_W = 64
D_FF = 4 * D_MODEL
H_R = 4
DK_R = 128
DV_R = 128
W_R = H_R * DV_R
H_A = 8
KV_A = 2
HEAD_DIM = 64
W_A = H_A * HEAD_DIM
ROPE_AXIS_DIM = HEAD_DIM // 2
ROPE_THETA = 10000.0
EPS = 1e-6
IN_W = 3 * H_R * DK_R + 2 * W_R + W_A + 2 * KV_A * HEAD_DIM
MOD_W = 6 * D_MODEL
MOD_ROWS = 8

LANES = 128
VMEM_LIMIT = 56 * 1024 * 1024

HGRN_CHUNK = 128
ATTN_QROWS = 256
FFN_CHUNK = 1024
ROW_TILE = 512


def _dot(a, b):
    return jnp.dot(a, b, preferred_element_type=F32)


def _dot_nt(a, b):
    return lax.dot_general(a, b, (((1,), (1,)), ((), ())), preferred_element_type=F32)


def _dot_tn(a, b):
    return lax.dot_general(a, b, (((0,), (0,)), ((), ())), preferred_element_type=F32)


def _params(*sem):
    return pltpu.CompilerParams(dimension_semantics=sem, vmem_limit_bytes=VMEM_LIMIT)


def _mod_kernel(c_ref, w_ref, b_ref, o_ref):
    c = c_ref[...]
    s = c / (1.0 + jnp.exp(-c))
    o_ref[...] = _dot(s.astype(BF16), w_ref[...].astype(BF16)) + b_ref[...]


def _modulation(cvec, w_mod, b_mod):
    tn = 1536
    return pl.pallas_call(
        _mod_kernel,
        out_shape=jax.ShapeDtypeStruct((DEPTH, MOD_ROWS, MOD_W), F32),
        grid=(DEPTH, MOD_W // tn),
        in_specs=[
            pl.BlockSpec((MOD_ROWS, D_MODEL), lambda l, j: (0, 0)),
            pl.BlockSpec((None, D_MODEL, tn), lambda l, j: (l, 0, j)),
            pl.BlockSpec((None, 1, tn), lambda l, j: (l, 0, j)),
        ],
        out_specs=pl.BlockSpec((None, MOD_ROWS, tn), lambda l, j: (l, 0, j)),
        compiler_params=_params("arbitrary", "arbitrary"),
        name="modulation",
    )(cvec, w_mod, b_mod.reshape(DEPTH, 1, MOD_W))


def _rms(x):
    return x * lax.rsqrt(jnp.mean(x * x, axis=-1, keepdims=True) + EPS)


def _inproj_kernel(x_ref, mod_ref, g_ref, w_ref, o_ref):
    y = _rms(x_ref[...]) * g_ref[...]
    h = y * (1.0 + mod_ref[:, D_MODEL:2 * D_MODEL]) + mod_ref[:, 0:D_MODEL]
    o_ref[...] = _dot(h.astype(BF16), w_ref[...])


def _mod_row_map(rows_per_seq, tile, ctx):
    if ctx:
        return lambda i: (MOD_ROWS // 2, 0, 0)
    per = rows_per_seq // tile
    return lambda i: (i // per, 0, 0)


def _inproj(x, mod_l, g, w, rows_per_seq, ctx):
    n = x.shape[0]
    tm = ROW_TILE
    return pl.pallas_call(
        _inproj_kernel,
        out_shape=jax.ShapeDtypeStruct((n, IN_W), F32),
        grid=(n // tm,),
        in_specs=[
            pl.BlockSpec((tm, D_MODEL), lambda i: (i, 0)),
            pl.BlockSpec((None, 1, MOD_W), _mod_row_map(rows_per_seq, tm, ctx)),
            pl.BlockSpec((1, D_MODEL), lambda i: (0, 0)),
            pl.BlockSpec((D_MODEL, IN_W), lambda i: (0, 0)),
        ],
        out_specs=pl.BlockSpec((tm, IN_W), lambda i: (i, 0)),
        compiler_params=_params("arbitrary"),
        name="inproj",
    )(x, mod_l, g, w)


def _bcast_block_row(b, blk, r):
    n, w = b.shape
    parts = [jnp.broadcast_to(b[k * blk + r:k * blk + r + 1, :], (blk, w)) for k in range(n // blk)]
    return parts[0] if len(parts) == 1 else jnp.concatenate(parts, axis=0)


def _hgrn_kernel(*refs, layer, seq, has_s0, want_s):
    L = HGRN_CHUNK
    n_chunks = seq // L
    n_levels = int(np.log2(L))
    refs = list(refs)
    q_ref, ff_ref, fb_ref, i_ref, g_ref, lbraw_ref, gn_ref = refs[:7]
    pos = 7
    s0_ref = None
    if has_s0:
        s0_ref = refs[pos]
        pos += 1
    o_ref = refs[pos]
    pos += 1
    s_ref = None
    if want_s:
        s_ref = refs[pos]
        pos += 1
    of_ref, ob_ref, st_ref, code_ref = refs[pos:pos + 4]

    if layer > 0:
        raw = lbraw_ref[...]
        e = jnp.exp(raw - jnp.max(raw, axis=0, keepdims=True))
        p = e / jnp.sum(e, axis=0, keepdims=True)
        lb = p[1]
        for j in range(2, layer + 1):
            lb = lb + p[j]
        log_lb = jnp.log(lb)
        log_1m_lb = jnp.log1p(-lb)
        one_m_lb = 1.0 - lb

    def gates(fx, d):
        ls = jnp.minimum(fx, 0.0) - jnp.log1p(jnp.exp(-jnp.abs(fx)))
        k = jnp.exp(ls - fx)
        if layer == 0:
            return ls, k
        a = log_lb[d:d + 1, :]
        c = log_1m_lb[d:d + 1, :] + ls
        lf = jnp.maximum(a, c) + jnp.log1p(jnp.exp(-jnp.abs(a - c)))
        return lf, one_m_lb[d:d + 1, :] * k

    ti = lax.broadcasted_iota(jnp.int32, (L, L), 0)
    si = lax.broadcasted_iota(jnp.int32, (L, L), 1)
    x = ti ^ si
    lv = jnp.zeros((L, L), jnp.int32)
    for j in range(n_levels):
        lv = lv + jnp.where((x >> j) != 0, 1, 0)
    code_ref[0] = jnp.where(ti >= si, lv, -1)
    code_ref[1] = jnp.where(ti <= si, lv, -1)

    for d in range(2):
        for h in range(H_R):
            if has_s0:
                st_ref[d, h] = s0_ref[d, h].T
            else:
                st_ref[d, h] = jnp.zeros((DV_R, DK_R), F32)

    def direction(d, c, f_ref, out_ref):
        rows = pl.ds(pl.multiple_of(c * L, L), L)
        row = lax.broadcasted_iota(jnp.int32, (L, W_R), 0)
        code = code_ref[d]
        tri = jnp.where(code >= 0, 1.0, 0.0).astype(BF16)
        q = q_ref[rows, :]
        v = i_ref[rows, :]
        lf, k = gates(f_ref[rows, :], d)
        hi = lf.astype(BF16)
        r1 = lf - hi.astype(F32)
        mid = r1.astype(BF16)
        lo = (r1 - mid.astype(F32)).astype(BF16)
        b3 = _dot(tri, jnp.concatenate([hi, mid, lo], axis=1))
        b = b3[:, 0:W_R] + b3[:, W_R:2 * W_R] + b3[:, 2 * W_R:3 * W_R]

        qb = q.astype(BF16)
        kb = k.astype(BF16)
        vb = v.astype(BF16)
        a_h = []
        for h in range(H_R):
            cs = slice(h * DK_R, (h + 1) * DK_R)
            a_h.append(jnp.where(code == 0, _dot_nt(qb[:, cs], kb[:, cs]), 0.0))
        for j in range(1, n_levels + 1):
            half = 1 << (j - 1)
            blk = 2 * half
            u = row & (blk - 1)
            if j == 1:
                e_j = jnp.where(u == (1 - d), lf, 0.0)
            elif j == 2:
                up = pltpu.roll(lf, L - 1, axis=0)
                dn = pltpu.roll(lf, 1, axis=0)
                if d == 0:
                    e_j = jnp.where(u == 0, up, jnp.where(u == 1, 0.0, jnp.where(u == 2, lf, lf + dn)))
                else:
                    e_j = jnp.where(u == 0, lf + up, jnp.where(u == 1, lf, jnp.where(u == 2, 0.0, dn)))
            else:
                diff = b - _bcast_block_row(b, blk, half - 1 + d)
                if d == 0:
                    e_j = jnp.where(u >= half, diff, -diff)
                else:
                    e_j = jnp.where(u < half, diff, -diff)
            w = jnp.exp(e_j)
            qj = (q * w).astype(BF16)
            kj = (k * w).astype(BF16)
            for h in range(H_R):
                cs = slice(h * DK_R, (h + 1) * DK_R)
                a_h[h] = jnp.where(code == j, _dot_nt(qj[:, cs], kj[:, cs]), a_h[h])

        edge = b[L - 1:L, :] if d == 0 else b[0:1, :]
        q_in = (q * jnp.exp(b)).astype(BF16)
        k_out = (k * jnp.exp(edge - b)).astype(BF16)
        dec = jnp.exp(edge)
        outs = []
        for h in range(H_R):
            cs = slice(h * DK_R, (h + 1) * DK_R)
            st = st_ref[d, h]
            o = _dot(a_h[h].astype(BF16), vb[:, cs]) + _dot_nt(q_in[:, cs], st.astype(BF16))
            st_ref[d, h] = st * dec[:, cs] + _dot(v[:, cs].T.astype(BF16), k_out[:, cs])
            outs.append(o)
        out_ref[rows, :] = jnp.concatenate(outs, axis=1)

    def step(c, carry):
        direction(0, c, ff_ref, of_ref)
        direction(1, n_chunks - 1 - c, fb_ref, ob_ref)
        return carry

    lax.fori_loop(0, n_chunks, step, 0)

    def finish(c, carry):
        rows = pl.ds(pl.multiple_of(c * L, L), L)
        o = of_ref[rows, :] + ob_ref[rows, :]
        g = g_ref[rows, :]
        parts = [_rms(o[:, h * DV_R:(h + 1) * DV_R]) for h in range(H_R)]
        y = jnp.concatenate(parts, axis=1) * gn_ref[...]
        o_ref[rows, :] = y * (g / (1.0 + jnp.exp(-g)))
        return carry

    lax.fori_loop(0, n_chunks, finish, 0)

    if want_s:
        for d in range(2):
            for h in range(H_R):
                s_ref[d, h] = st_ref[d, h].T


def _hgrn(proj, lb_raw, gn, s0, layer, batch, seq, want_s):
    has_s0 = s0 is not None
    col = lambda j: pl.BlockSpec((seq, W_R), lambda b, j=j: (b, j))
    in_specs = [col(0), col(1), col(2), col(3), col(4),
                pl.BlockSpec((DEPTH, 2, W_R), lambda b: (0, 0, 0)),
                pl.BlockSpec((1, W_R), lambda b: (0, 0))]
    args = [proj, proj, proj, proj, proj, lb_raw, gn]
    if has_s0:
        in_specs.append(pl.BlockSpec((None, None, 2, H_R, DK_R, DV_R), lambda b: (b, layer, 0, 0, 0, 0)))
        args.append(s0)
    out_shape = [jax.ShapeDtypeStruct((batch * seq, W_R), F32)]
    out_specs = [pl.BlockSpec((seq, W_R), lambda b: (b, 0))]
    if want_s:
        out_shape.append(jax.ShapeDtypeStruct((batch, 2, H_R, DK_R, DV_R), F32))
        out_specs.append(pl.BlockSpec((None, 2, H_R, DK_R, DV_R), lambda b: (b, 0, 0, 0, 0)))
    res = pl.pallas_call(
        functools.partial(_hgrn_kernel, layer=layer, seq=seq, has_s0=has_s0, want_s=want_s),
        out_shape=out_shape,
        grid=(batch,),
        in_specs=in_specs,
        out_specs=out_specs,
        scratch_shapes=[pltpu.VMEM((seq, W_R), F32), pltpu.VMEM((seq, W_R), F32),
                        pltpu.VMEM((2, H_R, DV_R, DK_R), F32),
                        pltpu.VMEM((2, HGRN_CHUNK, HGRN_CHUNK), jnp.int32)],
        compiler_params=_params("arbitrary"),
        name="hgrn",
    )(*args)
    return res if want_s else (res[0], None)


def _head_mean_square(x):
    w = x.shape[1]
    shift = HEAD_DIM.bit_length() - 1
    r = lax.broadcasted_iota(jnp.int32, (w, w), 0) >> shift
    c = lax.broadcasted_iota(jnp.int32, (w, w), 1) >> shift
    ones = jnp.where(r == c, 1.0, 0.0).astype(BF16)
    return _dot((x * x).astype(BF16), ones) * (1.0 / HEAD_DIM)


def _attn_kernel(*refs, seq, past):
    refs = list(refs)
    q_ref, kv_ref, qg_ref, kg_ref = refs[:4]
    pos = 4
    if past:
        ck_ref, cv_ref, cos_ref, sa_ref, sb_ref = refs[pos:pos + 5]
        pos += 5
        o_ref = refs[pos]
        pos += 1
    else:
        o_ref, kn_ref, vn_ref = refs[pos:pos + 3]
        pos += 3
    qs_ref, ks_ref, vs_ref = refs[pos:pos + 3]
    total = past + seq

    def rope(x):
        up = pltpu.roll(x, LANES - ROPE_AXIS_DIM // 2, axis=1)
        dn = pltpu.roll(x, ROPE_AXIS_DIM // 2, axis=1)
        return x * cos_ref[...] + up * sa_ref[...] + dn * sb_ref[...]

    q = q_ref[...]
    qn = q * lax.rsqrt(_head_mean_square(q) + EPS) * qg_ref[...]
    k = kv_ref[:, 0:LANES]
    v = kv_ref[:, LANES:2 * LANES]
    kn = k * lax.rsqrt(_head_mean_square(k) + EPS) * kg_ref[...]
    scale = HEAD_DIM ** -0.5
    qlow = lax.broadcasted_iota(jnp.int32, (seq, LANES), 1) < HEAD_DIM
    for p in range(W_A // LANES):
        cs = slice(p * LANES, (p + 1) * LANES)
        qp = (rope(qn[:, cs]) if past else qn[:, cs]) * scale
        qs_ref[0, :, cs] = jnp.where(qlow, qp, 0.0).astype(BF16)
        qs_ref[1, :, cs] = jnp.where(qlow, 0.0, qp).astype(BF16)
    if past:
        k_all = jnp.concatenate([ck_ref[...], rope(kn)], axis=0)
        v_all = jnp.concatenate([cv_ref[...], v], axis=0)
    else:
        k_all, v_all = kn, v
        kn_ref[...] = kn
        vn_ref[...] = v

    lane = lax.broadcasted_iota(jnp.int32, (total, LANES), 1)
    k_sw = pltpu.roll(k_all, HEAD_DIM, axis=1)
    v_sw = pltpu.roll(v_all, HEAD_DIM, axis=1)
    low = lane < HEAD_DIM
    ks_ref[0] = jnp.where(low, k_all, k_sw).astype(BF16)
    ks_ref[1] = jnp.where(low, k_sw, k_all).astype(BF16)
    vs_ref[0] = jnp.where(low, v_all, v_sw).astype(BF16)
    vs_ref[1] = jnp.where(low, v_sw, v_all).astype(BF16)

    tq = ATTN_QROWS
    olow = lax.broadcasted_iota(jnp.int32, (tq, LANES), 1) < HEAD_DIM
    pairs_per_kv = (W_A // LANES) // KV_A

    def tile(r, carry):
        rows = pl.ds(pl.multiple_of(r * tq, tq), tq)
        for p in range(W_A // LANES):
            kh = p // pairs_per_kv
            cs = slice(p * LANES, (p + 1) * LANES)
            q2 = jnp.concatenate([qs_ref[0, rows, cs], qs_ref[1, rows, cs]], axis=0)
            s = _dot_nt(q2, ks_ref[kh])
            pexp = jnp.exp(s - jnp.max(s, axis=-1, keepdims=True))
            den = jnp.sum(pexp, axis=-1, keepdims=True)
            o2 = _dot(pexp.astype(BF16), vs_ref[kh]) / den
            o_ref[rows, cs] = jnp.where(olow, o2[0:tq], o2[tq:2 * tq])
        return carry

    lax.fori_loop(0, seq // tq, tile, 0)


def _attn(proj, qg, kg, batch, seq, layer, cache=None, rope_tabs=None):
    past = 0 if cache is None else cache[0].shape[2]
    n_q = IN_W // W_A - 1
    n_kv = IN_W // (2 * LANES) - 1
    in_specs = [pl.BlockSpec((seq, W_A), lambda b: (b, n_q)),
                pl.BlockSpec((seq, 2 * LANES), lambda b: (b, n_kv)),
                pl.BlockSpec((1, W_A), lambda b: (0, 0)),
                pl.BlockSpec((1, LANES), lambda b: (0, 0))]
    args = [proj, proj, qg, kg]
    out_shape = [jax.ShapeDtypeStruct((batch * seq, W_A), F32)]
    out_specs = [pl.BlockSpec((seq, W_A), lambda b: (b, 0))]
    if past:
        ck, cv = cache
        cspec = pl.BlockSpec((None, None, past, LANES), lambda b: (b, layer, 0, 0))
        tspec = pl.BlockSpec((seq, LANES), lambda b: (0, 0))
        in_specs += [cspec, cspec, tspec, tspec, tspec]
        args += [ck, cv, *rope_tabs]
    else:
        for _ in range(2):
            out_shape.append(jax.ShapeDtypeStruct((batch, seq, LANES), F32))
            out_specs.append(pl.BlockSpec((None, seq, LANES), lambda b: (b, 0, 0)))
    res = pl.pallas_call(
        functools.partial(_attn_kernel, seq=seq, past=past),
        out_shape=out_shape,
        grid=(batch,),
        in_specs=in_specs,
        out_specs=out_specs,
        scratch_shapes=[pltpu.VMEM((2, seq, W_A), BF16),
                        pltpu.VMEM((KV_A, past + seq, LANES), BF16),
                        pltpu.VMEM((KV_A, past + seq, LANES), BF16)],
        compiler_params=_params("arbitrary"),
        name="attn",
    )(*args)
    return res


def _outffn_kernel(*refs, final):
    refs = list(refs)
    x_ref, or_ref, oa_ref, mod_ref, g2_ref, wout_ref, w1_ref, w2_ref = refs[:8]
    pos = 8
    gf_ref = None
    if final:
        gf_ref = refs[pos]
        pos += 1
    o_ref = refs[pos]
    gate1 = mod_ref[:, 2 * D_MODEL:3 * D_MODEL]
    shift2 = mod_ref[:, 3 * D_MODEL:4 * D_MODEL]
    scale2 = mod_ref[:, 4 * D_MODEL:5 * D_MODEL]
    gate2 = mod_ref[:, 5 * D_MODEL:6 * D_MODEL]
    mix = _dot(or_ref[...].astype(BF16), wout_ref[0:W_R, :]) + _dot(oa_ref[...].astype(BF16), wout_ref[W_R:W_R + W_A, :])
    x1 = x_ref[...] + gate1 * mix
    h2 = (_rms(x1) * g2_ref[...] * (1.0 + scale2) + shift2).astype(BF16)
    acc = jnp.zeros_like(x1)
    for j in range(D_FF // FFN_CHUNK):
        cs = slice(j * FFN_CHUNK, (j + 1) * FFN_CHUNK)
        hid = jnp.maximum(_dot(h2, w1_ref[:, cs]), 0.0)
        acc = acc + _dot((hid * hid).astype(BF16), w2_ref[cs, :])
    x2 = x1 + gate2 * acc
    if final:
        x2 = _rms(x2) * gf_ref[...]
    o_ref[...] = x2


def _outffn(x, o_r, o_a, mod_l, g2, wout, w1, w2, gf, rows_per_seq, ctx):
    n = x.shape[0]
    tm = ROW_TILE
    final = gf is not None
    row = lambda w: pl.BlockSpec((tm, w), lambda i: (i, 0))
    full = lambda a: pl.BlockSpec(a.shape, lambda i: (0,) * a.ndim)
    in_specs = [row(D_MODEL), row(W_R), row(W_A),
                pl.BlockSpec((None, 1, MOD_W), _mod_row_map(rows_per_seq, tm, ctx)),
                full(g2), full(wout), full(w1), full(w2)]
    args = [x, o_r, o_a, mod_l, g2, wout, w1, w2]
    if final:
        in_specs.append(full(gf))
        args.append(gf)
    return pl.pallas_call(
        functools.partial(_outffn_kernel, final=final),
        out_shape=jax.ShapeDtypeStruct((n, D_MODEL), F32),
        grid=(n // tm,),
        in_specs=in_specs,
        out_specs=row(D_MODEL),
        compiler_params=_params("arbitrary"),
        name="outffn",
    )(*args)


def _rope_tables(seq):
    rows = seq // GRID_W
    rowi = jnp.repeat(jnp.arange(rows, dtype=F32), GRID_W)
    coli = jnp.tile(jnp.arange(GRID_W, dtype=F32), rows)
    inv = ROPE_THETA ** (-jnp.arange(0, ROPE_AXIS_DIM, 2, dtype=F32) / ROPE_AXIS_DIM)
    ar = rowi[:, None] * inv[None, :]
    ac = coli[:, None] * inv[None, :]
    ang = jnp.concatenate([ar, ar, ac, ac], axis=-1)
    cos = jnp.tile(jnp.cos(ang), (1, LANES // HEAD_DIM))
    sin = jnp.tile(jnp.sin(ang), (1, LANES // HEAD_DIM))
    first = (jnp.arange(LANES) % ROPE_AXIS_DIM) < ROPE_AXIS_DIM // 2
    return cos, jnp.where(first, -sin, 0.0), jnp.where(first, 0.0, sin)


def kernel(x_prompt, x_sample, cache_k, cache_v, state_hgrn, c, c_ctx, w_mod, b_mod, norm1_g, w_in, lb_raw,
           hgrn_norm_g, q_norm_g, k_norm_g, w_out, norm2_g, w1, w2, final_norm_g):
    bp, sp, _ = x_prompt.shape
    bs, ss, _ = x_sample.shape
    past = cache_k.shape[2]

    cvec = jnp.zeros((MOD_ROWS, D_MODEL), F32).at[0:bs].set(c).at[MOD_ROWS // 2].set(c_ctx)
    mod = _modulation(cvec, w_mod, b_mod).reshape(DEPTH, MOD_ROWS, 1, MOD_W)

    w_in_b, w_out_b, w1_b, w2_b = (w.astype(BF16) for w in (w_in, w_out, w1, w2))
    ck = cache_k.reshape(bs, DEPTH, past, KV_A * HEAD_DIM)
    cv = cache_v.reshape(bs, DEPTH, past, KV_A * HEAD_DIM)
    tabs = _rope_tables(ss)
    gf = final_norm_g.reshape(1, D_MODEL)

    xp = x_prompt.reshape(bp * sp, D_MODEL)
    xs = x_sample.reshape(bs * ss, D_MODEL)
    new_k, new_v, new_s = [], [], []
    for l in range(DEPTH):
        g1 = norm1_g[l].reshape(1, D_MODEL)
        g2 = norm2_g[l].reshape(1, D_MODEL)
        gn = jnp.tile(hgrn_norm_g[l], H_R).reshape(1, W_R)
        qg = jnp.tile(q_norm_g[l], H_A).reshape(1, W_A)
        kg = jnp.tile(k_norm_g[l], KV_A).reshape(1, LANES)
        last = gf if l == DEPTH - 1 else None

        proj = _inproj(xp, mod[l], g1, w_in_b[l], sp, True)
        o_r, s_l = _hgrn(proj, lb_raw, gn, None, l, bp, sp, True)
        o_a, k_l, v_l = _attn(proj, qg, kg, bp, sp, l)
        xp = _outffn(xp, o_r, o_a, mod[l], g2, w_out_b[l], w1_b[l], w2_b[l], last, sp, True)
        new_k.append(k_l)
        new_v.append(v_l)
        new_s.append(s_l)

        proj = _inproj(xs, mod[l], g1, w_in_b[l], ss, False)
        o_r, _ = _hgrn(proj, lb_raw, gn, state_hgrn, l, bs, ss, False)
        o_a, = _attn(proj, qg, kg, bs, ss, l, (ck, cv), tabs)
        xs = _outffn(xs, o_r, o_a, mod[l], g2, w_out_b[l], w1_b[l], w2_b[l], last, ss, False)

    y_prompt = xp.reshape(bp, sp, D_MODEL)
    y_sample = xs.reshape(bs, ss, D_MODEL)
    new_k = jnp.stack(new_k, axis=1).reshape(bp, DEPTH, sp, KV_A, HEAD_DIM)
    new_v = jnp.stack(new_v, axis=1).reshape(bp, DEPTH, sp, KV_A, HEAD_DIM)
    new_s = jnp.stack(new_s, axis=1)
    return (y_prompt, y_sample, new_k, new_v, new_s)
```

```python
import functools

import jax
import jax.numpy as jnp
import numpy as np
from jax import lax
from jax.experimental import pallas as pl
from jax.experimental.pallas import tpu as pltpu

F32 = jnp.float32
BF16 = jnp.bfloat16

D_MODEL = 1024
DEPTH = 4
GRID_W = 64
D_FF = 4 * D_MODEL
H_R = 4
DK_R = 128
DV_R = 128
W_R = H_R * DV_R
H_A = 8
KV_A = 2
HEAD_DIM = 64
W_A = H_A * HEAD_DIM
ROPE_AXIS_DIM = HEAD_DIM // 2
ROPE_THETA = 10000.0
EPS = 1e-6
LOG2E = 1.4426950408889634
IN_W =3 * H_R * DK_R + 2 * W_R + W_A + 2 * KV_A * HEAD_DIM
MOD_W = 6 * D_MODEL
MOD_ROWS = 8

LANES = 128
SUBLANES = 8
VMEM_LIMIT = 56 * 1024 * 1024

HGRN_CHUNK = 128
ATTN_QROWS = 256
FFN_CHUNK = 1024
ROW_TILE = 512


def _dot(a, b):
    return jnp.dot(a, b, preferred_element_type=F32)


def _dot_nt(a, b):
    return lax.dot_general(a, b, (((1,), (1,)), ((), ())), preferred_element_type=F32)


def _dot_tn(a, b):
    return lax.dot_general(a, b, (((0,), (0,)), ((), ())), preferred_element_type=F32)


def _params(*sem):
    return pltpu.CompilerParams(dimension_semantics=sem, vmem_limit_bytes=VMEM_LIMIT)


def _mod_kernel(c_ref, w_ref, b_ref, o_ref):
    c = c_ref[...]
    s = c / (1.0 + jnp.exp(-c))
    o_ref[...] = _dot(s.astype(BF16), w_ref[...].astype(BF16)) + b_ref[...]


def _modulation(cvec, w_mod, b_mod):
    tn = 1536
    return pl.pallas_call(
        _mod_kernel,
        out_shape=jax.ShapeDtypeStruct((DEPTH, MOD_ROWS, MOD_W), F32),
        grid=(DEPTH, MOD_W // tn),
        in_specs=[
            pl.BlockSpec((MOD_ROWS, D_MODEL), lambda l, j: (0, 0)),
            pl.BlockSpec((None, D_MODEL, tn), lambda l, j: (l, 0, j)),
            pl.BlockSpec((None, 1, tn), lambda l, j: (l, 0, j)),
        ],
        out_specs=pl.BlockSpec((None, MOD_ROWS, tn), lambda l, j: (l, 0, j)),
        compiler_params=_params("arbitrary", "arbitrary"),
        name="modulation",
    )(cvec, w_mod, b_mod.reshape(DEPTH, 1, MOD_W))


def _rms(x):
    return x * lax.rsqrt(jnp.mean(x * x, axis=-1, keepdims=True) + EPS)


def _inproj_kernel(x_ref, mod_ref, g_ref, w_ref, o_ref):
    y = _rms(x_ref[...]) * g_ref[...]
    h = y * (1.0 + mod_ref[:, D_MODEL:2 * D_MODEL]) + mod_ref[:, 0:D_MODEL]
    o_ref[...] = _dot(h.astype(BF16), w_ref[...])


def _mod_spec(layer, rows_per_seq, tile, ctx):
    per = rows_per_seq // tile
    row = (lambda i: MOD_ROWS // 2) if ctx else (lambda i: i // per)
    return pl.BlockSpec((None, None, 1, MOD_W), lambda i: (layer, row(i), 0, 0))


def _layer_spec(a, layer):
    return pl.BlockSpec((None,) + a.shape[1:], lambda i: (layer, 0, 0))


def _inproj(x, mod, g, w, layer, rows_per_seq, ctx):
    n = x.shape[0]
    tm = ROW_TILE
    return pl.pallas_call(
        _inproj_kernel,
        out_shape=jax.ShapeDtypeStruct((n, IN_W), F32),
        grid=(n // tm,),
        in_specs=[
            pl.BlockSpec((tm, D_MODEL), lambda i: (i, 0)),
            _mod_spec(layer, rows_per_seq, tm, ctx),
            _layer_spec(g, layer),
            _layer_spec(w, layer),
        ],
        out_specs=pl.BlockSpec((tm, IN_W), lambda i: (i, 0)),
        compiler_params=_params("arbitrary"),
        name="inproj",
    )(x, mod, g, w)


def _bcast_block_row(b, blk, r):
    n, w = b.shape
    parts = [jnp.broadcast_to(b[k * blk + r:k * blk + r + 1, :], (blk, w)) for k in range(n // blk)]
    return parts[0] if len(parts) == 1 else jnp.concatenate(parts, axis=0)


def _hgrn_kernel(*refs, layer, seq, has_s0, has_prev, want_s):
    L = HGRN_CHUNK
    n_chunks = seq // L
    n_levels = int(np.log2(L))
    refs = list(refs)
    q_ref, ff_ref, fb_ref, i_ref, g_ref, lbraw_ref, gn_ref = refs[:7]
    pos = 7
    s0_ref = None
    if has_s0:
        s0_ref = refs[pos]
        pos += 1
    if has_prev:
        pos += 1
    o_ref = refs[pos]
    pos += 1
    s_ref = None
    if want_s:
        s_ref = refs[pos]
        pos += 1
    of_ref, ob_ref, st_ref, code_ref = refs[pos:pos + 4]

    if layer > 0:
        raw = lbraw_ref[...]
        e = jnp.exp(raw - jnp.max(raw, axis=0, keepdims=True))
        p = e / jnp.sum(e, axis=0, keepdims=True)
        lb = p[1]
        for j in range(2, layer + 1):
            lb = lb + p[j]
        log_lb = jnp.log(lb)
        log_1m_lb = jnp.log1p(-lb)
        one_m_lb = 1.0 - lb

    def gates(fx, d):
        ls = jnp.minimum(fx, 0.0) - jnp.log(1.0 + jnp.exp(-jnp.abs(fx)))
        k = jnp.exp(ls - fx)
        if layer == 0:
            return ls, k
        a = log_lb[d:d + 1, :]
        c = log_1m_lb[d:d + 1, :] + ls
        lf = jnp.maximum(a, c) + jnp.log(1.0 + jnp.exp(-jnp.abs(a - c)))
        return lf, one_m_lb[d:d + 1, :] * k

    ti = lax.broadcasted_iota(jnp.int32, (L, L), 0)
    si = lax.broadcasted_iota(jnp.int32, (L, L), 1)
    x = ti ^ si
    lv = jnp.zeros((L, L), jnp.int32)
    for j in range(n_levels):
        lv = lv + jnp.where((x >> j) != 0, 1, 0)
    code_ref[0] = jnp.where(ti >= si, lv, -1)
    code_ref[1] = jnp.where(ti <= si, lv, -1)

    for d in range(2):
        for h in range(H_R):
            if has_s0:
                st_ref[d, h] = s0_ref[d, h].T
            else:
                st_ref[d, h] = jnp.zeros((DV_R, DK_R), F32)

    def direction(d, c, f_ref, out_ref):
        rows = pl.ds(pl.multiple_of(c * L, L), L)
        row = lax.broadcasted_iota(jnp.int32, (L, W_R), 0)
        code = code_ref[d]
        tri = jnp.where(code >= 0, 1.0, 0.0).astype(BF16)
        q = q_ref[rows, :]
        v = i_ref[rows, :]
        lf, k = gates(f_ref[rows, :], d)
        lf = lf * LOG2E
        hi = lf.astype(BF16)
        r1 = lf - hi.astype(F32)
        mid = r1.astype(BF16)
        lo = (r1 - mid.astype(F32)).astype(BF16)
        b3 = _dot(tri, jnp.concatenate([hi, mid, lo], axis=1))
        b = b3[:, 0:W_R] + b3[:, W_R:2 * W_R] + b3[:, 2 * W_R:3 * W_R]

        vb = v.astype(BF16)
        a_h = [None] * H_R
        for j in range(1, n_levels + 1):
            half = 1 << (j - 1)
            blk = 2 * half
            u = row & (blk - 1)
            qside = (u >= half) if d == 0 else (u < half)
            if j == 1:
                e_j = jnp.where(qside, lf, 0.0)
            elif j == 2:
                up = pltpu.roll(lf, L - 1, axis=0)
                dn = pltpu.roll(lf, 1, axis=0)
                if d == 0:
                    e_j = jnp.where(u == 0, up, jnp.where(u == 1, 0.0, jnp.where(u == 2, lf, lf + dn)))
                else:
                    e_j = jnp.where(u == 0, lf + up, jnp.where(u == 1, lf, jnp.where(u == 2, 0.0, dn)))
            elif half < SUBLANES:
                diff = b - _bcast_block_row(b, blk, half - 1 + d)
                e_j = jnp.where(qside, diff, -diff)
            if half < SUBLANES:
                src = jnp.where(qside, q, k)
            else:
                e_parts, s_parts = [], []
                for i in range(L // half):
                    rs = slice(i * half, (i + 1) * half)
                    r = (i // 2) * blk + half - 1 + d
                    b_mid = jnp.broadcast_to(b[r:r + 1, :], (half, W_R))
                    is_q = (i % 2) == 1 - d
                    e_parts.append(b[rs, :] - b_mid if is_q else b_mid - b[rs, :])
                    s_parts.append((q if is_q else k)[rs, :])
                e_j = jnp.concatenate(e_parts, axis=0)
                src = jnp.concatenate(s_parts, axis=0)
            z = (src * jnp.exp2(e_j)).astype(BF16)
            for h in range(H_R):
                cs = slice(h * DK_R, (h + 1) * DK_R)
                p_j = _dot_nt(z[:, cs], z[:, cs])
                a_h[h] = jnp.where(code == j, p_j, 0.0 if a_h[h] is None else a_h[h])

        edge = b[L - 1:L, :] if d == 0 else b[0:1, :]
        q_in = (q * jnp.exp2(b)).astype(BF16)
        k_out = (k * jnp.exp2(edge - b)).astype(BF16)
        dec = jnp.exp2(edge)
        qk = q * k
        outs = []
        for h in range(H_R):
            cs = slice(h * DK_R, (h + 1) * DK_R)
            st = st_ref[d, h]
            o = _dot(a_h[h].astype(BF16), vb[:, cs]) + _dot_nt(q_in[:, cs], st.astype(BF16))
            o = o + jnp.sum(qk[:, cs], axis=-1, keepdims=True) * v[:, cs]
            st_ref[d, h] = st * dec[:, cs] + _dot(v[:, cs].T.astype(BF16), k_out[:, cs])
            outs.append(o)
        out_ref[rows, :] = jnp.concatenate(outs, axis=1)

    def step(c, carry):
        direction(0, c, ff_ref, of_ref)
        direction(1, n_chunks - 1 - c, fb_ref, ob_ref)
        return carry

    lax.fori_loop(0, n_chunks, step, 0)

    def finish(c, carry):
        rows = pl.ds(pl.multiple_of(c * L, L), L)
        o = of_ref[rows, :] + ob_ref[rows, :]
        g = g_ref[rows, :]
        parts = [_rms(o[:, h * DV_R:(h + 1) * DV_R]) for h in range(H_R)]
        y = jnp.concatenate(parts, axis=1) * gn_ref[...]
        o_ref[rows, :] = y * (g / (1.0 + jnp.exp(-g)))
        return carry

    lax.fori_loop(0, n_chunks, finish, 0)

    if want_s:
        for d in range(2):
            for h in range(H_R):
                s_ref[d, h] = st_ref[d, h].T


def _hgrn(proj, lb_raw, gn, layer, batch, seq, s0=None, s_prev=None, want_s=False):
    has_s0 = s0 is not None
    col = lambda j: pl.BlockSpec((seq, W_R), lambda b, j=j: (b, j))
    state_spec = pl.BlockSpec((None, None, 2, H_R, DK_R, DV_R), lambda b: (b, layer, 0, 0, 0, 0))
    in_specs = [col(0), col(1), col(2), col(3), col(4),
                pl.BlockSpec((DEPTH, 2, W_R), lambda b: (0, 0, 0)),
                _layer_spec(gn, layer)]
    args = [proj, proj, proj, proj, proj, lb_raw, gn]
    if has_s0:
        in_specs.append(state_spec)
        args.append(s0)
    out_shape = [jax.ShapeDtypeStruct((batch * seq, W_R), F32)]
    out_specs = [pl.BlockSpec((seq, W_R), lambda b: (b, 0))]
    aliases = {}
    if want_s:
        out_shape.append(jax.ShapeDtypeStruct((batch, DEPTH, 2, H_R, DK_R, DV_R), F32))
        out_specs.append(state_spec)
        if s_prev is not None:
            aliases[len(args)] = 1
            in_specs.append(pl.BlockSpec(memory_space=pl.ANY))
            args.append(s_prev)
    res = pl.pallas_call(
        functools.partial(_hgrn_kernel, layer=layer, seq=seq, has_s0=has_s0, has_prev=bool(aliases),
                          want_s=want_s),
        out_shape=out_shape,
        grid=(batch,),
        in_specs=in_specs,
        out_specs=out_specs,
        scratch_shapes=[pltpu.VMEM((seq, W_R), F32), pltpu.VMEM((seq, W_R), F32),
                        pltpu.VMEM((2, H_R, DV_R, DK_R), F32),
                        pltpu.VMEM((2, HGRN_CHUNK, HGRN_CHUNK), jnp.int32)],
        input_output_aliases=aliases,
        compiler_params=_params("arbitrary"),
        name="hgrn",
    )(*args)
    return res if want_s else (res[0], None)


def _head_mean_square(x):
    w = x.shape[1]
    shift = HEAD_DIM.bit_length() - 1
    r = lax.broadcasted_iota(jnp.int32, (w, w), 0) >> shift
    c = lax.broadcasted_iota(jnp.int32, (w, w), 1) >> shift
    ones = jnp.where(r == c, 1.0, 0.0).astype(BF16)
    return _dot((x * x).astype(BF16), ones) * (1.0 / HEAD_DIM)


def _attn_kernel(*refs, seq, past, n_prev):
    refs = list(refs)
    q_ref, kv_ref, qg_ref, kg_ref = refs[:4]
    pos = 4
    if past:
        ck_ref, cv_ref, cos_ref, sa_ref, sb_ref = refs[pos:pos + 5]
        pos += 5
        o_ref = refs[pos]
        pos += 1
    else:
        pos += n_prev
        o_ref, kn_ref, vn_ref = refs[pos:pos + 3]
        pos += 3
    qs_ref, ks_ref, vs_ref = refs[pos:pos + 3]
    total = past + seq

    def rope(x):
        up = pltpu.roll(x, LANES - ROPE_AXIS_DIM // 2, axis=1)
        dn = pltpu.roll(x, ROPE_AXIS_DIM // 2, axis=1)
        return x * cos_ref[...] + up * sa_ref[...] + dn * sb_ref[...]

    q = q_ref[...]
    qn = q * lax.rsqrt(_head_mean_square(q) + EPS) * qg_ref[...]
    k = kv_ref[:, 0:LANES]
    v = kv_ref[:, LANES:2 * LANES]
    kn = k * lax.rsqrt(_head_mean_square(k) + EPS) * kg_ref[...]
    scale = HEAD_DIM ** -0.5
    qlow = lax.broadcasted_iota(jnp.int32, (seq, LANES), 1) < HEAD_DIM
    for p in range(W_A // LANES):
        cs = slice(p * LANES, (p + 1) * LANES)
        qp = (rope(qn[:, cs]) if past else qn[:, cs]) * scale
        qs_ref[0, :, cs] = jnp.where(qlow, qp, 0.0).astype(BF16)
        qs_ref[1, :, cs] = jnp.where(qlow, 0.0, qp).astype(BF16)
    if past:
        k_all = jnp.concatenate([ck_ref[...], rope(kn)], axis=0)
        v_all = jnp.concatenate([cv_ref[...], v], axis=0)
    else:
        k_all, v_all = kn, v
        kn_ref[...] = kn
        vn_ref[...] = v

    lane = lax.broadcasted_iota(jnp.int32, (total, LANES), 1)
    k_sw = pltpu.roll(k_all, HEAD_DIM, axis=1)
    v_sw = pltpu.roll(v_all, HEAD_DIM, axis=1)
    low = lane < HEAD_DIM
    ks_ref[0] = jnp.where(low, k_all, k_sw).astype(BF16)
    ks_ref[1] = jnp.where(low, k_sw, k_all).astype(BF16)
    vs_ref[0] = jnp.where(low, v_all, v_sw).astype(BF16)
    vs_ref[1] = jnp.where(low, v_sw, v_all).astype(BF16)

    tq = ATTN_QROWS
    olow = lax.broadcasted_iota(jnp.int32, (tq, LANES), 1) < HEAD_DIM
    pairs_per_kv = (W_A // LANES) // KV_A

    def tile(r, carry):
        rows = pl.ds(pl.multiple_of(r * tq, tq), tq)
        for p in range(W_A // LANES):
            kh = p // pairs_per_kv
            cs = slice(p * LANES, (p + 1) * LANES)
            q2 = jnp.concatenate([qs_ref[0, rows, cs], qs_ref[1, rows, cs]], axis=0)
            s = _dot_nt(q2, ks_ref[kh])
            pexp = jnp.exp(s - jnp.max(s, axis=-1, keepdims=True))
            den = jnp.sum(pexp, axis=-1, keepdims=True)
            o2 = _dot(pexp.astype(BF16), vs_ref[kh]) / den
            o_ref[rows, cs] = jnp.where(olow, o2[0:tq], o2[tq:2 * tq])
        return carry

    lax.fori_loop(0, seq // tq, tile, 0)


def _attn(proj, qg, kg, batch, seq, layer, cache=None, rope_tabs=None, kv_prev=None):
    past = 0 if cache is None else cache[0].shape[2]
    n_q = IN_W // W_A - 1
    n_kv = IN_W // (2 * LANES) - 1
    in_specs = [pl.BlockSpec((seq, W_A), lambda b: (b, n_q)),
                pl.BlockSpec((seq, 2 * LANES), lambda b: (b, n_kv)),
                _layer_spec(qg, layer), _layer_spec(kg, layer)]
    args = [proj, proj, qg, kg]
    out_shape = [jax.ShapeDtypeStruct((batch * seq, W_A), F32)]
    out_specs = [pl.BlockSpec((seq, W_A), lambda b: (b, 0))]
    aliases = {}
    if past:
        ck, cv = cache
        cspec = pl.BlockSpec((None, None, past, LANES), lambda b: (b, layer, 0, 0))
        tspec = pl.BlockSpec((seq, LANES), lambda b: (0, 0))
        in_specs += [cspec, cspec, tspec, tspec, tspec]
        args += [ck, cv, *rope_tabs]
    else:
        for i in range(2):
            out_shape.append(jax.ShapeDtypeStruct((batch, DEPTH, seq, LANES), F32))
            out_specs.append(pl.BlockSpec((None, None, seq, LANES), lambda b: (b, layer, 0, 0)))
            if kv_prev is not None:
                aliases[len(args)] = 1 + i
                in_specs.append(pl.BlockSpec(memory_space=pl.ANY))
                args.append(kv_prev[i])
    res = pl.pallas_call(
        functools.partial(_attn_kernel, seq=seq, past=past, n_prev=len(aliases)),
        out_shape=out_shape,
        grid=(batch,),
        in_specs=in_specs,
        out_specs=out_specs,
        scratch_shapes=[pltpu.VMEM((2, seq, W_A), BF16),
                        pltpu.VMEM((KV_A, past + seq, LANES), BF16),
                        pltpu.VMEM((KV_A, past + seq, LANES), BF16)],
        input_output_aliases=aliases,
        compiler_params=_params("arbitrary"),
        name="attn",
    )(*args)
    return res


def _outffn_kernel(*refs, final):
    refs = list(refs)
    x_ref, or_ref, oa_ref, mod_ref, g2_ref, wout_ref, w1_ref, w2_ref = refs[:8]
    pos = 8
    gf_ref = None
    if final:
        gf_ref = refs[pos]
        pos += 1
    o_ref = refs[pos]
    gate1 = mod_ref[:, 2 * D_MODEL:3 * D_MODEL]
    shift2 = mod_ref[:, 3 * D_MODEL:4 * D_MODEL]
    scale2 = mod_ref[:, 4 * D_MODEL:5 * D_MODEL]
    gate2 = mod_ref[:, 5 * D_MODEL:6 * D_MODEL]
    mix = _dot(or_ref[...].astype(BF16), wout_ref[0:W_R, :]) + _dot(oa_ref[...].astype(BF16), wout_ref[W_R:W_R + W_A, :])
    x1 = x_ref[...] + gate1 * mix
    h2 = (_rms(x1) * g2_ref[...] * (1.0 + scale2) + shift2).astype(BF16)
    acc = jnp.zeros_like(x1)
    for j in range(D_FF // FFN_CHUNK):
        cs = slice(j * FFN_CHUNK, (j + 1) * FFN_CHUNK)
        hid = jnp.maximum(_dot(h2, w1_ref[:, cs]), 0.0)
        acc = acc + _dot((hid * hid).astype(BF16), w2_ref[cs, :])
    x2 = x1 + gate2 * acc
    if final:
        x2 = _rms(x2) * gf_ref[...]
    o_ref[...] = x2


def _outffn(x, o_r, o_a, mod, g2, wout, w1, w2, gf, layer, rows_per_seq, ctx):
    n = x.shape[0]
    tm = ROW_TILE
    final = gf is not None
    row = lambda w: pl.BlockSpec((tm, w), lambda i: (i, 0))
    in_specs = [row(D_MODEL), row(W_R), row(W_A),
                _mod_spec(layer, rows_per_seq, tm, ctx),
                _layer_spec(g2, layer), _layer_spec(wout, layer), _layer_spec(w1, layer), _layer_spec(w2, layer)]
    args = [x, o_r, o_a, mod, g2, wout, w1, w2]
    if final:
        in_specs.append(pl.BlockSpec(gf.shape, lambda i: (0, 0)))
        args.append(gf)
    return pl.pallas_call(
        functools.partial(_outffn_kernel, final=final),
        out_shape=jax.ShapeDtypeStruct((n, D_MODEL), F32),
        grid=(n // tm,),
        in_specs=in_specs,
        out_specs=row(D_MODEL),
        compiler_params=_params("arbitrary"),
        name="outffn",
    )(*args)


def _rope_tables(seq):
    rows = seq // GRID_W
    rowi = jnp.repeat(jnp.arange(rows, dtype=F32), GRID_W)
    coli = jnp.tile(jnp.arange(GRID_W, dtype=F32), rows)
    inv = ROPE_THETA ** (-jnp.arange(0, ROPE_AXIS_DIM, 2, dtype=F32) / ROPE_AXIS_DIM)
    ar = rowi[:, None] * inv[None, :]
    ac = coli[:, None] * inv[None, :]
    ang = jnp.concatenate([ar, ar, ac, ac], axis=-1)
    cos = jnp.tile(jnp.cos(ang), (1, LANES // HEAD_DIM))
    sin = jnp.tile(jnp.sin(ang), (1, LANES // HEAD_DIM))
    first = (jnp.arange(LANES) % ROPE_AXIS_DIM) < ROPE_AXIS_DIM // 2
    return cos, jnp.where(first, -sin, 0.0), jnp.where(first, 0.0, sin)


def kernel(x_prompt, x_sample, cache_k, cache_v, state_hgrn, c, c_ctx, w_mod, b_mod, norm1_g, w_in, lb_raw,
           hgrn_norm_g, q_norm_g, k_norm_g, w_out, norm2_g, w1, w2, final_norm_g):
    bp, sp, _ = x_prompt.shape
    bs, ss, _ = x_sample.shape
    past = cache_k.shape[2]

    cvec = jnp.zeros((MOD_ROWS, D_MODEL), F32).at[0:bs].set(c).at[MOD_ROWS // 2].set(c_ctx)
    mod = _modulation(cvec, w_mod, b_mod).reshape(DEPTH, MOD_ROWS, 1, MOD_W)

    w_in_b, w_out_b, w1_b, w2_b = (w.astype(BF16) for w in (w_in, w_out, w1, w2))
    ck = cache_k.reshape(bs, DEPTH, past, KV_A * HEAD_DIM)
    cv = cache_v.reshape(bs, DEPTH, past, KV_A * HEAD_DIM)
    tabs = _rope_tables(ss)
    gf = final_norm_g.reshape(1, D_MODEL)

    g1 = norm1_g.reshape(DEPTH, 1, D_MODEL)
    g2 = norm2_g.reshape(DEPTH, 1, D_MODEL)
    gn = jnp.tile(hgrn_norm_g, (1, H_R)).reshape(DEPTH, 1, W_R)
    qg = jnp.tile(q_norm_g, (1, H_A)).reshape(DEPTH, 1, W_A)
    kg = jnp.tile(k_norm_g, (1, KV_A)).reshape(DEPTH, 1, LANES)

    xp = x_prompt.reshape(bp * sp, D_MODEL)
    xs = x_sample.reshape(bs * ss, D_MODEL)
    new_kv, new_s = None, None
    for l in range(DEPTH):
        last = gf if l == DEPTH - 1 else None

        proj = _inproj(xp, mod, g1, w_in_b, l, sp, True)
        o_r, new_s = _hgrn(proj, lb_raw, gn, l, bp, sp, s_prev=new_s, want_s=True)
        o_a, *new_kv = _attn(proj, qg, kg, bp, sp, l, kv_prev=new_kv)
        xp = _outffn(xp, o_r, o_a, mod, g2, w_out_b, w1_b, w2_b, last, l, sp, True)

        proj = _inproj(xs, mod, g1, w_in_b, l, ss, False)
        o_r, _ = _hgrn(proj, lb_raw, gn, l, bs, ss, s0=state_hgrn)
        o_a, = _attn(proj, qg, kg, bs, ss, l, cache=(ck, cv), rope_tabs=tabs)
        xs = _outffn(xs, o_r, o_a, mod, g2, w_out_b, w1_b, w2_b, last, l, ss, False)

    y_prompt = xp.reshape(bp, sp, D_MODEL)
    y_sample = xs.reshape(bs, ss, D_MODEL)
    new_k = new_kv[0].reshape(bp, DEPTH, sp, KV_A, HEAD_DIM)
    new_v = new_kv[1].reshape(bp, DEPTH, sp, KV_A, HEAD_DIM)
    return (y_prompt, y_sample, new_k, new_v, new_s)
```

```python
import functools

import jax
import jax.numpy as jnp
import numpy as np
from jax import lax
from jax.experimental import pallas as pl
from jax.experimental.pallas import tpu as pltpu

F32 = jnp.float32
BF16 = jnp.bfloat16

D_MODEL = 1024
DEPTH = 4
GRID_W = 64
D_FF = 4 * D_MODEL
H_R = 4
DK_R = 128
DV_R = 128
W_R = H_R * DV_R
H_A = 8
KV_A = 2
HEAD_DIM = 64
W_A = H_A * HEAD_DIM
ROPE_AXIS_DIM = HEAD_DIM // 2
ROPE_THETA = 10000.0
EPS = 1e-6
LOG2E = 1.4426950408889634
IN_R = 3 * H_R * DK_R + 2 * W_R
IN_A = W_A + 2 * KV_A * HEAD_DIM
MOD_W = 6 * D_MODEL
MOD_ROWS = 8

LANES = 128
SUBLANES = 8
VMEM_LIMIT = 56 * 1024 * 1024

HGRN_CHUNK = 128
PROJ_ROWS = 256
ATTN_QROWS = 256
ATTN_KEYS = 256
ATTN_LOOKAHEAD = 2
FFN_CHUNK = 1024
ROW_TILE = 512


def _dot(a, b):
    return jnp.dot(a, b, preferred_element_type=F32)


def _dot_nt(a, b):
    return lax.dot_general(a, b, (((1,), (1,)), ((), ())), preferred_element_type=F32)


def _params(*sem):
    return pltpu.CompilerParams(dimension_semantics=sem, vmem_limit_bytes=VMEM_LIMIT)


def _rms(x):
    return x * lax.rsqrt(jnp.mean(x * x, axis=-1, keepdims=True) + EPS)


def _mod_spec(layer, rows_per_seq, tile, ctx):
    per = rows_per_seq // tile
    row = (lambda i: MOD_ROWS // 2) if ctx else (lambda i: i // per)
    return pl.BlockSpec((None, None, 1, MOD_W), lambda i: (layer, row(i), 0, 0))


def _layer_spec(a, layer):
    return pl.BlockSpec((None,) + a.shape[1:], lambda i: (layer, 0, 0))


def _prenorm(x, shift, scale, g):
    return (_rms(x) * g * (1.0 + scale) + shift).astype(BF16)


def _mod_kernel(c_ref, w_ref, b_ref, o_ref):
    c = c_ref[...]
    s = c / (1.0 + jnp.exp(-c))
    o_ref[...] = _dot(s.astype(BF16), w_ref[...].astype(BF16)) + b_ref[...]


def _modulation(cvec, w_mod, b_mod):
    tn = 1536
    return pl.pallas_call(
        _mod_kernel,
        out_shape=jax.ShapeDtypeStruct((DEPTH, MOD_ROWS, MOD_W), F32),
        grid=(DEPTH, MOD_W // tn),
        in_specs=[
            pl.BlockSpec((MOD_ROWS, D_MODEL), lambda l, j: (0, 0)),
            pl.BlockSpec((None, D_MODEL, tn), lambda l, j: (l, 0, j)),
            pl.BlockSpec((None, 1, tn), lambda l, j: (l, 0, j)),
        ],
        out_specs=pl.BlockSpec((None, MOD_ROWS, tn), lambda l, j: (l, 0, j)),
        compiler_params=_params("arbitrary", "arbitrary"),
        name="modulation",
    )(cvec, w_mod, b_mod.reshape(DEPTH, 1, MOD_W))


def _prenorm_kernel(x_ref, mod_ref, g_ref, o_ref):
    o_ref[...] = _prenorm(x_ref[...], mod_ref[:, 0:D_MODEL], mod_ref[:, D_MODEL:2 * D_MODEL], g_ref[...])


def _first_prenorm(x, mod, g1, rows_per_seq, ctx):
    n = x.shape[0]
    tm = ROW_TILE
    return pl.pallas_call(
        _prenorm_kernel,
        out_shape=jax.ShapeDtypeStruct((n, D_MODEL), BF16),
        grid=(n // tm,),
        in_specs=[pl.BlockSpec((tm, D_MODEL), lambda i: (i, 0)),
                  _mod_spec(0, rows_per_seq, tm, ctx),
                  _layer_spec(g1, 0)],
        out_specs=pl.BlockSpec((tm, D_MODEL), lambda i: (i, 0)),
        compiler_params=_params("arbitrary"),
        name="prenorm",
    )(x, mod, g1)


def _bcast_block_row(b, blk, r):
    n, w = b.shape
    parts = [jnp.broadcast_to(b[k * blk + r:k * blk + r + 1, :], (blk, w)) for k in range(n // blk)]
    return parts[0] if len(parts) == 1 else jnp.concatenate(parts, axis=0)


def _hgrn_kernel(*refs, layer, seq, has_s0, has_prev, want_s):
    L = HGRN_CHUNK
    n_chunks = seq // L
    n_levels = int(np.log2(L))
    refs = list(refs)
    h_ref, w_ref, lbraw_ref, gn_ref = refs[:4]
    pos = 4
    s0_ref = None
    if has_s0:
        s0_ref = refs[pos]
        pos += 1
    if has_prev:
        pos += 1
    o_ref = refs[pos]
    pos += 1
    s_ref = None
    if want_s:
        s_ref = refs[pos]
        pos += 1
    proj_ref, of_ref, ob_ref, st_ref, code_ref = refs[pos:pos + 5]
    q_cols, i_cols, g_cols = (slice(n * W_R, (n + 1) * W_R) for n in (0, 3, 4))
    f_cols = [slice(W_R, 2 * W_R), slice(2 * W_R, 3 * W_R)]

    def project(c, carry):
        rows = pl.ds(pl.multiple_of(c * PROJ_ROWS, PROJ_ROWS), PROJ_ROWS)
        proj_ref[rows, :] = _dot(h_ref[rows, :], w_ref[...])
        return carry

    lax.fori_loop(0, seq // PROJ_ROWS, project, 0)

    if layer > 0:
        raw = lbraw_ref[...]
        e = jnp.exp(raw - jnp.max(raw, axis=0, keepdims=True))
        p = e / jnp.sum(e, axis=0, keepdims=True)
        lb = p[1]
        for j in range(2, layer + 1):
            lb = lb + p[j]
        log_lb = jnp.log(lb)
        log_1m_lb = jnp.log1p(-lb)
        one_m_lb = 1.0 - lb

    def gates(fx, d):
        ls = jnp.minimum(fx, 0.0) - jnp.log(1.0 + jnp.exp(-jnp.abs(fx)))
        k = jnp.exp(ls - fx)
        if layer == 0:
            return ls, k
        a = log_lb[d:d + 1, :]
        c = log_1m_lb[d:d + 1, :] + ls
        lf = jnp.maximum(a, c) + jnp.log(1.0 + jnp.exp(-jnp.abs(a - c)))
        return lf, one_m_lb[d:d + 1, :] * k

    ti = lax.broadcasted_iota(jnp.int32, (L, L), 0)
    si = lax.broadcasted_iota(jnp.int32, (L, L), 1)
    x = ti ^ si
    lv = jnp.zeros((L, L), jnp.int32)
    for j in range(n_levels):
        lv = lv + jnp.where((x >> j) != 0, 1, 0)
    code_ref[0] = jnp.where(ti >= si, lv, -1)
    code_ref[1] = jnp.where(ti <= si, lv, -1)

    for d in range(2):
        for h in range(H_R):
            if has_s0:
                st_ref[d, h] = s0_ref[d, h].T
            else:
                st_ref[d, h] = jnp.zeros((DV_R, DK_R), F32)

    def direction(d, c, out_ref):
        rows = pl.ds(pl.multiple_of(c * L, L), L)
        row = lax.broadcasted_iota(jnp.int32, (L, W_R), 0)
        code = code_ref[d]
        tri = jnp.where(code >= 0, 1.0, 0.0).astype(BF16)
        q = proj_ref[rows, q_cols]
        v = proj_ref[rows, i_cols]
        lf, k = gates(proj_ref[rows, f_cols[d]], d)
        lf = lf * LOG2E
        hi = lf.astype(BF16)
        r1 = lf - hi.astype(F32)
        mid = r1.astype(BF16)
        lo = (r1 - mid.astype(F32)).astype(BF16)
        b3 = _dot(tri, jnp.concatenate([hi, mid, lo], axis=1))
        b = b3[:, 0:W_R] + b3[:, W_R:2 * W_R] + b3[:, 2 * W_R:3 * W_R]
        yield

        vb = v.astype(BF16)
        a_h = [None] * H_R
        for j in range(1, n_levels + 1):
            half = 1 << (j - 1)
            blk = 2 * half
            u = row & (blk - 1)
            qside = (u >= half) if d == 0 else (u < half)
            if j == 1:
                e_j = jnp.where(qside, lf, 0.0)
            elif j == 2:
                up = pltpu.roll(lf, L - 1, axis=0)
                dn = pltpu.roll(lf, 1, axis=0)
                if d == 0:
                    e_j = jnp.where(u == 0, up, jnp.where(u == 1, 0.0, jnp.where(u == 2, lf, lf + dn)))
                else:
                    e_j = jnp.where(u == 0, lf + up, jnp.where(u == 1, lf, jnp.where(u == 2, 0.0, dn)))
            else:
                diff = b - _bcast_block_row(b, blk, half - 1 + d)
                e_j = jnp.where(qside, diff, -diff)
            z = (jnp.where(qside, q, k) * jnp.exp2(e_j)).astype(BF16)
            for h in range(H_R):
                cs = slice(h * DK_R, (h + 1) * DK_R)
                p_j = _dot_nt(z[:, cs], z[:, cs])
                a_h[h] = jnp.where(code == j, p_j, 0.0 if a_h[h] is None else a_h[h])
            yield

        edge = b[L - 1:L, :] if d == 0 else b[0:1, :]
        q_in = (q * jnp.exp2(b)).astype(BF16)
        k_out = (k * jnp.exp2(edge - b)).astype(BF16)
        dec = jnp.exp2(edge)
        qk = q * k
        outs = []
        for h in range(H_R):
            cs = slice(h * DK_R, (h + 1) * DK_R)
            st = st_ref[d, h]
            o = _dot(a_h[h].astype(BF16), vb[:, cs]) + _dot_nt(q_in[:, cs], st.astype(BF16))
            o = o + jnp.sum(qk[:, cs], axis=-1, keepdims=True) * v[:, cs]
            st_ref[d, h] = st * dec[:, cs] + _dot(v[:, cs].T.astype(BF16), k_out[:, cs])
            outs.append(o)
        out_ref[rows, :] = jnp.concatenate(outs, axis=1)

    def step(c, carry):
        active = [direction(0, c, of_ref), direction(1, n_chunks - 1 - c, ob_ref)]
        while active:
            for gen in list(active):
                if next(gen, "done") == "done":
                    active.remove(gen)
        return carry

    lax.fori_loop(0, n_chunks, step, 0)

    def finish(c, carry):
        rows = pl.ds(pl.multiple_of(c * L, L), L)
        o = of_ref[rows, :] + ob_ref[rows, :]
        g = proj_ref[rows, g_cols]
        parts = [_rms(o[:, h * DV_R:(h + 1) * DV_R]) for h in range(H_R)]
        y = jnp.concatenate(parts, axis=1) * gn_ref[...]
        o_ref[rows, :] = y * (g / (1.0 + jnp.exp(-g)))
        return carry

    lax.fori_loop(0, n_chunks, finish, 0)

    if want_s:
        for d in range(2):
            for h in range(H_R):
                s_ref[d, h] = st_ref[d, h].T


def _hgrn(hb, w_r, lb_raw, gn, layer, batch, seq, s0=None, s_prev=None, want_s=False):
    has_s0 = s0 is not None
    state_spec = pl.BlockSpec((None, None, 2, H_R, DK_R, DV_R), lambda b: (b, layer, 0, 0, 0, 0))
    in_specs = [pl.BlockSpec((seq, D_MODEL), lambda b: (b, 0)),
                _layer_spec(w_r, layer),
                pl.BlockSpec((DEPTH, 2, W_R), lambda b: (0, 0, 0)),
                _layer_spec(gn, layer)]
    args = [hb, w_r, lb_raw, gn]
    if has_s0:
        in_specs.append(state_spec)
        args.append(s0)
    out_shape = [jax.ShapeDtypeStruct((batch * seq, W_R), F32)]
    out_specs = [pl.BlockSpec((seq, W_R), lambda b: (b, 0))]
    aliases = {}
    if want_s:
        out_shape.append(jax.ShapeDtypeStruct((batch, DEPTH, 2, H_R, DK_R, DV_R), F32))
        out_specs.append(state_spec)
        if s_prev is not None:
            aliases[len(args)] = 1
            in_specs.append(pl.BlockSpec(memory_space=pl.ANY))
            args.append(s_prev)
    res = pl.pallas_call(
        functools.partial(_hgrn_kernel, layer=layer, seq=seq, has_s0=has_s0, has_prev=bool(aliases),
                          want_s=want_s),
        out_shape=out_shape,
        grid=(batch,),
        in_specs=in_specs,
        out_specs=out_specs,
        scratch_shapes=[pltpu.VMEM((seq, IN_R), F32),
                        pltpu.VMEM((seq, W_R), F32), pltpu.VMEM((seq, W_R), F32),
                        pltpu.VMEM((2, H_R, DV_R, DK_R), F32),
                        pltpu.VMEM((2, HGRN_CHUNK, HGRN_CHUNK), jnp.int32)],
        input_output_aliases=aliases,
        compiler_params=_params("arbitrary"),
        name="hgrn",
    )(*args)
    return res if want_s else (res[0], None)


def _head_mean_square(x):
    w = x.shape[1]
    shift = HEAD_DIM.bit_length() - 1
    r = lax.broadcasted_iota(jnp.int32, (w, w), 0) >> shift
    c = lax.broadcasted_iota(jnp.int32, (w, w), 1) >> shift
    ones = jnp.where(r == c, 1.0, 0.0).astype(BF16)
    return _dot((x * x).astype(BF16), ones) * (1.0 / HEAD_DIM)


def _attn_kernel(*refs, seq, past, n_prev):
    refs = list(refs)
    h_ref, w_ref, qg_ref, kg_ref = refs[:4]
    pos = 4
    if past:
        ck_ref, cv_ref, cos_ref, sa_ref, sb_ref = refs[pos:pos + 5]
        pos += 5
        o_ref = refs[pos]
        pos += 1
    else:
        pos += n_prev
        o_ref, kn_ref, vn_ref = refs[pos:pos + 3]
        pos += 3
    qs_ref, ks_ref, vt_ref = refs[pos:pos + 3]

    def rope(x):
        up = pltpu.roll(x, LANES - ROPE_AXIS_DIM // 2, axis=1)
        dn = pltpu.roll(x, ROPE_AXIS_DIM // 2, axis=1)
        return x * cos_ref[...] + up * sa_ref[...] + dn * sb_ref[...]

    proj = _dot(h_ref[...], w_ref[...])
    q = proj[:, 0:W_A]
    k = proj[:, W_A:W_A + LANES]
    v = proj[:, W_A + LANES:W_A + 2 * LANES]
    qn = q * lax.rsqrt(_head_mean_square(q) + EPS) * qg_ref[...]
    kn = k * lax.rsqrt(_head_mean_square(k) + EPS) * kg_ref[...]
    scale = HEAD_DIM ** -0.5 * LOG2E
    low = lax.broadcasted_iota(jnp.int32, (seq, LANES), 1) < HEAD_DIM
    pairs_per_kv = (W_A // LANES) // KV_A
    for p in range(W_A // LANES):
        cs = slice(p * LANES, (p + 1) * LANES)
        qp = (rope(qn[:, cs]) if past else qn[:, cs]) * scale
        even = jnp.where(low, qp, 0.0)
        odd = jnp.where(low, 0.0, qp)
        if p // pairs_per_kv == 0:
            odd = pltpu.roll(odd, HEAD_DIM, axis=1)
        else:
            even = pltpu.roll(even, HEAD_DIM, axis=1)
        qs_ref[0, :, cs] = even.astype(BF16)
        qs_ref[1, :, cs] = odd.astype(BF16)
    if past:
        ks_ref[...] = jnp.concatenate([ck_ref[...], rope(kn)], axis=0).astype(BF16)
        vt_ref[...] = jnp.concatenate([cv_ref[...], v], axis=0).T.astype(BF16)
    else:
        ks_ref[...] = kn.astype(BF16)
        vt_ref[...] = v.T.astype(BF16)
        kn_ref[...] = kn
        vn_ref[...] = v

    tq = ATTN_QROWS
    total = past + seq

    n_blocks = total // ATTN_KEYS
    steps = [(p, kc) for p in range(W_A // LANES) for kc in range(n_blocks)]

    def tile(r, carry):
        rows = pl.ds(pl.multiple_of(r * tq, tq), tq)

        def scores(step):
            p, kc = step
            cs = slice(p * LANES, (p + 1) * LANES)
            q2 = jnp.concatenate([qs_ref[0, rows, cs], qs_ref[1, rows, cs]], axis=0)
            return _dot_nt(ks_ref[kc * ATTN_KEYS:(kc + 1) * ATTN_KEYS, :], q2)

        ahead = [scores(st) for st in steps[:ATTN_LOOKAHEAD]]
        for i, (p, kc) in enumerate(steps):
            s = ahead.pop(0)
            if i + ATTN_LOOKAHEAD < len(steps):
                ahead.append(scores(steps[i + ATTN_LOOKAHEAD]))
            kh = p // pairs_per_kv
            v_t = vt_ref[kh * HEAD_DIM:(kh + 1) * HEAD_DIM, kc * ATTN_KEYS:(kc + 1) * ATTN_KEYS]
            s_max = jnp.max(s, axis=0, keepdims=True)
            if kc == 0:
                m = s_max
                pexp = jnp.exp2(s - m)
                den = jnp.sum(pexp, axis=0, keepdims=True)
                o2 = _dot(v_t, pexp.astype(BF16))
            else:
                m_new = jnp.maximum(m, s_max)
                alpha = jnp.exp2(m - m_new)
                pexp = jnp.exp2(s - m_new)
                den = den * alpha + jnp.sum(pexp, axis=0, keepdims=True)
                o2 = o2 * alpha + _dot(v_t, pexp.astype(BF16))
                m = m_new
            if kc == n_blocks - 1:
                o2 = o2 / den
                o_ref[rows, p * LANES:(p + 1) * LANES] = jnp.concatenate([o2[:, 0:tq], o2[:, tq:2 * tq]], axis=0).T
        return carry

    lax.fori_loop(0, seq // tq, tile, 0)


def _attn(hb, w_a, qg, kg, batch, seq, layer, cache=None, rope_tabs=None, kv_prev=None):
    past = 0 if cache is None else cache[0].shape[2]
    in_specs = [pl.BlockSpec((seq, D_MODEL), lambda b: (b, 0)),
                _layer_spec(w_a, layer), _layer_spec(qg, layer), _layer_spec(kg, layer)]
    args = [hb, w_a, qg, kg]
    out_shape = [jax.ShapeDtypeStruct((batch * seq, W_A), F32)]
    out_specs = [pl.BlockSpec((seq, W_A), lambda b: (b, 0))]
    aliases = {}
    if past:
        ck, cv = cache
        cspec = pl.BlockSpec((None, None, past, LANES), lambda b: (b, layer, 0, 0))
        tspec = pl.BlockSpec((seq, LANES), lambda b: (0, 0))
        in_specs += [cspec, cspec, tspec, tspec, tspec]
        args += [ck, cv, *rope_tabs]
    else:
        for i in range(2):
            out_shape.append(jax.ShapeDtypeStruct((batch, DEPTH, seq, LANES), F32))
            out_specs.append(pl.BlockSpec((None, None, seq, LANES), lambda b: (b, layer, 0, 0)))
            if kv_prev is not None:
                aliases[len(args)] = 1 + i
                in_specs.append(pl.BlockSpec(memory_space=pl.ANY))
                args.append(kv_prev[i])
    res = pl.pallas_call(
        functools.partial(_attn_kernel, seq=seq, past=past, n_prev=len(aliases)),
        out_shape=out_shape,
        grid=(batch,),
        in_specs=in_specs,
        out_specs=out_specs,
        scratch_shapes=[pltpu.VMEM((2, seq, W_A), BF16),
                        pltpu.VMEM((past + seq, LANES), BF16),
                        pltpu.VMEM((LANES, past + seq), BF16)],
        input_output_aliases=aliases,
        compiler_params=_params("arbitrary"),
        name="attn",
    )(*args)
    return res


def _outffn_kernel(*refs, final):
    refs = list(refs)
    x_ref, or_ref, oa_ref, mod_ref, g2_ref, wout_ref, w1_ref, w2_ref, last_ref = refs[:9]
    pos = 9
    if not final:
        nmod_ref = refs[pos]
        pos += 1
    o_ref = refs[pos]
    gate1 = mod_ref[:, 2 * D_MODEL:3 * D_MODEL]
    shift2 = mod_ref[:, 3 * D_MODEL:4 * D_MODEL]
    scale2 = mod_ref[:, 4 * D_MODEL:5 * D_MODEL]
    gate2 = mod_ref[:, 5 * D_MODEL:6 * D_MODEL]
    mix = _dot(or_ref[...].astype(BF16), wout_ref[0:W_R, :]) + _dot(oa_ref[...].astype(BF16), wout_ref[W_R:W_R + W_A, :])
    x1 = x_ref[...] + gate1 * mix
    h2 = _prenorm(x1, shift2, scale2, g2_ref[...])
    acc = jnp.zeros_like(x1)
    for j in range(D_FF // FFN_CHUNK):
        cs = slice(j * FFN_CHUNK, (j + 1) * FFN_CHUNK)
        hid = jnp.maximum(_dot(h2, w1_ref[:, cs]), 0.0)
        acc = acc + _dot((hid * hid).astype(BF16), w2_ref[cs, :])
    x2 = x1 + gate2 * acc
    if final:
        o_ref[...] = _rms(x2) * last_ref[...]
    else:
        o_ref[...] = x2
        hn_ref = refs[pos + 1]
        hn_ref[...] = _prenorm(x2, nmod_ref[:, 0:D_MODEL], nmod_ref[:, D_MODEL:2 * D_MODEL], last_ref[...])


def _outffn(x, o_r, o_a, mod, g1, g2, wout, w1, w2, gf, layer, rows_per_seq, ctx):
    n = x.shape[0]
    tm = ROW_TILE
    final = layer == DEPTH - 1
    row = lambda w: pl.BlockSpec((tm, w), lambda i: (i, 0))
    in_specs = [row(D_MODEL), row(W_R), row(W_A),
                _mod_spec(layer, rows_per_seq, tm, ctx),
                _layer_spec(g2, layer), _layer_spec(wout, layer), _layer_spec(w1, layer), _layer_spec(w2, layer)]
    args = [x, o_r, o_a, mod, g2, wout, w1, w2]
    out_shape = [jax.ShapeDtypeStruct((n, D_MODEL), F32)]
    out_specs = [row(D_MODEL)]
    if final:
        in_specs.append(pl.BlockSpec(gf.shape, lambda i: (0, 0)))
        args.append(gf)
    else:
        in_specs += [_layer_spec(g1, layer + 1), _mod_spec(layer + 1, rows_per_seq, tm, ctx)]
        args += [g1, mod]
        out_shape.append(jax.ShapeDtypeStruct((n, D_MODEL), BF16))
        out_specs.append(row(D_MODEL))
    res = pl.pallas_call(
        functools.partial(_outffn_kernel, final=final),
        out_shape=out_shape,
        grid=(n // tm,),
        in_specs=in_specs,
        out_specs=out_specs,
        compiler_params=_params("arbitrary"),
        name="outffn",
    )(*args)
    return (res[0], None) if final else res


def _rope_tables(seq):
    rows = seq // GRID_W
    rowi = jnp.repeat(jnp.arange(rows, dtype=F32), GRID_W)
    coli = jnp.tile(jnp.arange(GRID_W, dtype=F32), rows)
    inv = ROPE_THETA ** (-jnp.arange(0, ROPE_AXIS_DIM, 2, dtype=F32) / ROPE_AXIS_DIM)
    ar = rowi[:, None] * inv[None, :]
    ac = coli[:, None] * inv[None, :]
    ang = jnp.concatenate([ar, ar, ac, ac], axis=-1)
    cos = jnp.tile(jnp.cos(ang), (1, LANES // HEAD_DIM))
    sin = jnp.tile(jnp.sin(ang), (1, LANES // HEAD_DIM))
    first = (jnp.arange(LANES) % ROPE_AXIS_DIM) < ROPE_AXIS_DIM // 2
    return cos, jnp.where(first, -sin, 0.0), jnp.where(first, 0.0, sin)


def kernel(x_prompt, x_sample, cache_k, cache_v, state_hgrn, c, c_ctx, w_mod, b_mod, norm1_g, w_in, lb_raw,
           hgrn_norm_g, q_norm_g, k_norm_g, w_out, norm2_g, w1, w2, final_norm_g):
    bp, sp, _ = x_prompt.shape
    bs, ss, _ = x_sample.shape
    past = cache_k.shape[2]

    cvec = jnp.zeros((MOD_ROWS, D_MODEL), F32).at[0:bs].set(c).at[MOD_ROWS // 2].set(c_ctx)
    mod = _modulation(cvec, w_mod, b_mod).reshape(DEPTH, MOD_ROWS, 1, MOD_W)

    w_r_b = w_in[:, :, :IN_R].astype(BF16)
    w_a_b = w_in[:, :, IN_R:].astype(BF16)
    w_out_b, w1_b, w2_b = (w.astype(BF16) for w in (w_out, w1, w2))
    ck = cache_k.reshape(bs, DEPTH, past, KV_A * HEAD_DIM)
    cv = cache_v.reshape(bs, DEPTH, past, KV_A * HEAD_DIM)
    tabs = _rope_tables(ss)
    gf = final_norm_g.reshape(1, D_MODEL)
    g1 = norm1_g.reshape(DEPTH, 1, D_MODEL)
    g2 = norm2_g.reshape(DEPTH, 1, D_MODEL)
    gn = jnp.tile(hgrn_norm_g, (1, H_R)).reshape(DEPTH, 1, W_R)
    qg = jnp.tile(q_norm_g, (1, H_A)).reshape(DEPTH, 1, W_A)
    kg = jnp.tile(k_norm_g, (1, KV_A)).reshape(DEPTH, 1, LANES)

    xp = x_prompt.reshape(bp * sp, D_MODEL)
    xs = x_sample.reshape(bs * ss, D_MODEL)
    hp = _first_prenorm(xp, mod, g1, sp, True)
    hs = _first_prenorm(xs, mod, g1, ss, False)
    new_kv, new_s = None, None
    for l in range(DEPTH):
        o_r, new_s = _hgrn(hp, w_r_b, lb_raw, gn, l, bp, sp, s_prev=new_s, want_s=True)
        o_a, *new_kv = _attn(hp, w_a_b, qg, kg, bp, sp, l, kv_prev=new_kv)
        xp, hp = _outffn(xp, o_r, o_a, mod, g1, g2, w_out_b, w1_b, w2_b, gf, l, sp, True)

        o_r, _ = _hgrn(hs, w_r_b, lb_raw, gn, l, bs, ss, s0=state_hgrn)
        o_a, = _attn(hs, w_a_b, qg, kg, bs, ss, l, cache=(ck, cv), rope_tabs=tabs)
        xs, hs = _outffn(xs, o_r, o_a, mod, g1, g2, w_out_b, w1_b, w2_b, gf, l, ss, False)

    y_prompt = xp.reshape(bp, sp, D_MODEL)
    y_sample = xs.reshape(bs, ss, D_MODEL)
    new_k = new_kv[0].reshape(bp, DEPTH, sp, KV_A, HEAD_DIM)
    new_v = new_kv[1].reshape(bp, DEPTH, sp, KV_A, HEAD_DIM)
    return (y_prompt, y_sample, new_k, new_v, new_s)
```

```python
import functools

import jax
import jax.numpy as jnp
import numpy as np
from jax import lax
from jax.experimental import pallas as pl
from jax.experimental.pallas import tpu as pltpu

F32 = jnp.float32
BF16 = jnp.bfloat16

D_MODEL = 1024
DEPTH = 4
GRID_W = 64
D_FF = 4 * D_MODEL
H_R = 4
DK_R = 128
DV_R = 128
W_R = H_R * DV_R
H_A = 8
KV_A = 2
HEAD_DIM = 64
W_A = H_A * HEAD_DIM
ROPE_AXIS_DIM = HEAD_DIM // 2
ROPE_THETA = 10000.0
EPS = 1e-6
LOG2E = 1.4426950408889634
IN_R = 3 * H_R * DK_R + 2 * W_R
IN_A = W_A + 2 * KV_A * HEAD_DIM
MOD_W = 6 * D_MODEL
MOD_ROWS = 8

LANES = 128
SUBLANES = 8
VMEM_LIMIT = 56 * 1024 * 1024

HGRN_CHUNK = 128
FAST_CHUNK = 64
FAST_UNROLL = 8
FAST_LIMIT = 115.0
PROJ_ROWS = 256
ATTN_QROWS = 256
ATTN_KEYS = 256
ATTN_LOOKAHEAD = 2
FFN_CHUNK = 1024
ROW_TILE = 512


def _dot(a, b):
    return jnp.dot(a, b, preferred_element_type=F32)


def _dot_nt(a, b):
    return lax.dot_general(a, b, (((1,), (1,)), ((), ())), preferred_element_type=F32)


def _params(*sem):
    return pltpu.CompilerParams(dimension_semantics=sem, vmem_limit_bytes=VMEM_LIMIT)


def _rms(x):
    return x * lax.rsqrt(jnp.mean(x * x, axis=-1, keepdims=True) + EPS)


def _mod_spec(layer, rows_per_seq, tile, ctx):
    per = rows_per_seq // tile
    row = (lambda i: MOD_ROWS // 2) if ctx else (lambda i: i // per)
    return pl.BlockSpec((None, None, 1, MOD_W), lambda i: (layer, row(i), 0, 0))


def _layer_spec(a, layer):
    return pl.BlockSpec((None,) + a.shape[1:], lambda i: (layer, 0, 0))


def _prenorm(x, shift, scale, g):
    return (_rms(x) * g * (1.0 + scale) + shift).astype(BF16)


def _mod_kernel(c_ref, w_ref, b_ref, o_ref):
    c = c_ref[...]
    s = c / (1.0 + jnp.exp(-c))
    o_ref[...] = _dot(s.astype(BF16), w_ref[...].astype(BF16)) + b_ref[...]


def _modulation(cvec, w_mod, b_mod):
    tn = 1536
    return pl.pallas_call(
        _mod_kernel,
        out_shape=jax.ShapeDtypeStruct((DEPTH, MOD_ROWS, MOD_W), F32),
        grid=(DEPTH, MOD_W // tn),
        in_specs=[
            pl.BlockSpec((MOD_ROWS, D_MODEL), lambda l, j: (0, 0)),
            pl.BlockSpec((None, D_MODEL, tn), lambda l, j: (l, 0, j)),
            pl.BlockSpec((None, 1, tn), lambda l, j: (l, 0, j)),
        ],
        out_specs=pl.BlockSpec((None, MOD_ROWS, tn), lambda l, j: (l, 0, j)),
        compiler_params=_params("arbitrary", "arbitrary"),
        name="modulation",
    )(cvec, w_mod, b_mod.reshape(DEPTH, 1, MOD_W))


def _prenorm_kernel(x_ref, mod_ref, g_ref, o_ref):
    o_ref[...] = _prenorm(x_ref[...], mod_ref[:, 0:D_MODEL], mod_ref[:, D_MODEL:2 * D_MODEL], g_ref[...])


def _first_prenorm(x, mod, g1, rows_per_seq, ctx):
    n = x.shape[0]
    tm = ROW_TILE
    return pl.pallas_call(
        _prenorm_kernel,
        out_shape=jax.ShapeDtypeStruct((n, D_MODEL), BF16),
        grid=(n // tm,),
        in_specs=[pl.BlockSpec((tm, D_MODEL), lambda i: (i, 0)),
                  _mod_spec(0, rows_per_seq, tm, ctx),
                  _layer_spec(g1, 0)],
        out_specs=pl.BlockSpec((tm, D_MODEL), lambda i: (i, 0)),
        compiler_params=_params("arbitrary"),
        name="prenorm",
    )(x, mod, g1)


def _bcast_block_row(b, blk, r):
    n, w = b.shape
    parts = [jnp.broadcast_to(b[k * blk + r:k * blk + r + 1, :], (blk, w)) for k in range(n // blk)]
    return parts[0] if len(parts) == 1 else jnp.concatenate(parts, axis=0)


def _hgrn_kernel(*refs, layer, seq, has_s0, has_prev, want_s):
    L = HGRN_CHUNK
    n_chunks = seq // L
    n_levels = int(np.log2(L))
    refs = list(refs)
    h_ref, w_ref, lbraw_ref, gn_ref = refs[:4]
    pos = 4
    s0_ref = None
    if has_s0:
        s0_ref = refs[pos]
        pos += 1
    if has_prev:
        pos += 1
    o_ref = refs[pos]
    pos += 1
    s_ref = None
    if want_s:
        s_ref = refs[pos]
        pos += 1
    proj_ref, of_ref, ob_ref, st_ref, code_ref, kg_ref, b_ref = refs[pos:pos + 7]
    q_cols, i_cols, g_cols = (slice(n * W_R, (n + 1) * W_R) for n in (0, 3, 4))
    f_cols = [slice(W_R, 2 * W_R), slice(2 * W_R, 3 * W_R)]

    def project(c, carry):
        rows = pl.ds(pl.multiple_of(c * PROJ_ROWS, PROJ_ROWS), PROJ_ROWS)
        proj_ref[rows, :] = _dot(h_ref[rows, :], w_ref[...])
        return carry

    lax.fori_loop(0, seq // PROJ_ROWS, project, 0)

    if layer > 0:
        raw = lbraw_ref[...]
        e = jnp.exp(raw - jnp.max(raw, axis=0, keepdims=True))
        p = e / jnp.sum(e, axis=0, keepdims=True)
        lb = p[1]
        for j in range(2, layer + 1):
            lb = lb + p[j]
        log_lb = jnp.log(lb)
        log_1m_lb = jnp.log1p(-lb)
        one_m_lb = 1.0 - lb

    def gates(fx, d):
        ls = jnp.minimum(fx, 0.0) - jnp.log(1.0 + jnp.exp(-jnp.abs(fx)))
        k = jnp.exp(ls - fx)
        if layer == 0:
            return ls, k
        a = log_lb[d:d + 1, :]
        c = log_1m_lb[d:d + 1, :] + ls
        lf = jnp.maximum(a, c) + jnp.log(1.0 + jnp.exp(-jnp.abs(a - c)))
        return lf, one_m_lb[d:d + 1, :] * k

    ti = lax.broadcasted_iota(jnp.int32, (L, L), 0)
    si = lax.broadcasted_iota(jnp.int32, (L, L), 1)
    x = ti ^ si
    lv = jnp.zeros((L, L), jnp.int32)
    for j in range(n_levels):
        lv = lv + jnp.where((x >> j) != 0, 1, 0)
    code_ref[0] = jnp.where(ti >= si, lv, -1)
    code_ref[1] = jnp.where(ti <= si, lv, -1)

    for d in range(2):
        for h in range(H_R):
            if has_s0:
                st_ref[d, h] = s0_ref[d, h].T
            else:
                st_ref[d, h] = jnp.zeros((DV_R, DK_R), F32)

    def cumulative(lf, tri):
        hi = lf.astype(BF16)
        r1 = lf - hi.astype(F32)
        mid = r1.astype(BF16)
        lo = (r1 - mid.astype(F32)).astype(BF16)
        b3 = _dot(tri, jnp.concatenate([hi, mid, lo], axis=1))
        return b3[:, 0:W_R] + b3[:, W_R:2 * W_R] + b3[:, 2 * W_R:3 * W_R]

    Lf = FAST_CHUNK
    n_fast = seq // Lf
    unroll = min(FAST_UNROLL, n_fast)
    ref_row = [Lf // 2 - 1, Lf // 2]
    first_row = [0, Lf - 1]
    last_row = [Lf - 1, 0]
    tf = lax.broadcasted_iota(jnp.int32, (Lf, Lf), 0)
    sf = lax.broadcasted_iota(jnp.int32, (Lf, Lf), 1)
    keep = [tf >= sf, tf <= sf]

    def prepare(c, worst):
        for i in range(unroll):
            rows = pl.ds(pl.multiple_of((c * unroll + i) * Lf, Lf), Lf)
            q_top = jnp.max(jnp.abs(proj_ref[rows, q_cols]), axis=0, keepdims=True)
            q_bits = jnp.log(jnp.maximum(q_top, 1.0)) * LOG2E
            for d in range(2):
                lf, k = gates(proj_ref[rows, f_cols[d]], d)
                b = cumulative(lf * LOG2E, jnp.where(keep[d], 1.0, 0.0).astype(BF16))
                kg_ref[d, rows, :] = k
                b_ref[d, rows, :] = b
                b_mid = b[ref_row[d]:ref_row[d] + 1, :]
                grow = jnp.maximum(b[first_row[d]:first_row[d] + 1, :] - b_mid,
                                   b_mid - b[last_row[d]:last_row[d] + 1, :])
                worst = jnp.maximum(worst, grow + q_bits)
        return worst

    worst = lax.fori_loop(0, n_fast // unroll, prepare, jnp.zeros((1, W_R), F32))

    def fast_direction(d, c, out_ref):
        rows = pl.ds(pl.multiple_of(c * Lf, Lf), Lf)
        q = proj_ref[rows, q_cols]
        v = proj_ref[rows, i_cols]
        k = kg_ref[d, rows, :]
        b = b_ref[d, rows, :]
        b_mid = b[ref_row[d]:ref_row[d] + 1, :]
        edge = b[last_row[d]:last_row[d] + 1, :]
        q_t = (q * jnp.exp2(b - b_mid)).astype(BF16)
        k_f = k * jnp.exp2(b_mid - b)
        k_t = k_f.astype(BF16)
        k_out = (k_f * jnp.exp2(edge - b_mid)).astype(BF16)
        dec = jnp.exp2(edge)
        from_start = jnp.exp2(b_mid)
        vb = v.astype(BF16)
        heads = [slice(h * DK_R, (h + 1) * DK_R) for h in range(H_R)]
        both = []
        for h, cs in enumerate(heads):
            st = st_ref[d, h]
            rhs = jnp.concatenate([(st * from_start[:, cs]).astype(BF16), k_t[:, cs]], axis=0)
            both.append(_dot_nt(q_t[:, cs], rhs))
            st_ref[d, h] = st * dec[:, cs] + _dot(v[:, cs].T.astype(BF16), k_out[:, cs])
        yield
        outs = []
        for h, cs in enumerate(heads):
            a = jnp.where(keep[d], both[h][:, DV_R:DV_R + Lf], 0.0).astype(BF16)
            outs.append(both[h][:, 0:DV_R] + _dot(a, vb[:, cs]))
        out_ref[rows, :] = jnp.concatenate(outs, axis=1)

    def in_turn(gens):
        active = list(gens)
        while active:
            for gen in list(active):
                if next(gen, "done") == "done":
                    active.remove(gen)

    def fast_step(c, carry):
        gens = []
        for i in range(unroll):
            gens.append(fast_direction(0, c * unroll + i, of_ref))
            gens.append(fast_direction(1, n_fast - 1 - (c * unroll + i), ob_ref))
        in_turn(gens)
        return carry

    def direction(d, c, out_ref):
        rows = pl.ds(pl.multiple_of(c * L, L), L)
        row = lax.broadcasted_iota(jnp.int32, (L, W_R), 0)
        code = code_ref[d]
        tri = jnp.where(code >= 0, 1.0, 0.0).astype(BF16)
        q = proj_ref[rows, q_cols]
        v = proj_ref[rows, i_cols]
        lf, k = gates(proj_ref[rows, f_cols[d]], d)
        lf = lf * LOG2E
        b = cumulative(lf, tri)
        yield

        vb = v.astype(BF16)
        a_h = [None] * H_R
        for j in range(1, n_levels + 1):
            half = 1 << (j - 1)
            blk = 2 * half
            u = row & (blk - 1)
            qside = (u >= half) if d == 0 else (u < half)
            if j == 1:
                e_j = jnp.where(qside, lf, 0.0)
            elif j == 2:
                up = pltpu.roll(lf, L - 1, axis=0)
                dn = pltpu.roll(lf, 1, axis=0)
                if d == 0:
                    e_j = jnp.where(u == 0, up, jnp.where(u == 1, 0.0, jnp.where(u == 2, lf, lf + dn)))
                else:
                    e_j = jnp.where(u == 0, lf + up, jnp.where(u == 1, lf, jnp.where(u == 2, 0.0, dn)))
            else:
                diff = b - _bcast_block_row(b, blk, half - 1 + d)
                e_j = jnp.where(qside, diff, -diff)
            z = (jnp.where(qside, q, k) * jnp.exp2(e_j)).astype(BF16)
            for h in range(H_R):
                cs = slice(h * DK_R, (h + 1) * DK_R)
                p_j = _dot_nt(z[:, cs], z[:, cs])
                a_h[h] = jnp.where(code == j, p_j, 0.0 if a_h[h] is None else a_h[h])
            yield

        edge = b[L - 1:L, :] if d == 0 else b[0:1, :]
        q_in = (q * jnp.exp2(b)).astype(BF16)
        k_out = (k * jnp.exp2(edge - b)).astype(BF16)
        dec = jnp.exp2(edge)
        qk = q * k
        outs = []
        for h in range(H_R):
            cs = slice(h * DK_R, (h + 1) * DK_R)
            st = st_ref[d, h]
            o = _dot(a_h[h].astype(BF16), vb[:, cs]) + _dot_nt(q_in[:, cs], st.astype(BF16))
            o = o + jnp.sum(qk[:, cs], axis=-1, keepdims=True) * v[:, cs]
            st_ref[d, h] = st * dec[:, cs] + _dot(v[:, cs].T.astype(BF16), k_out[:, cs])
            outs.append(o)
        out_ref[rows, :] = jnp.concatenate(outs, axis=1)

    def step(c, carry):
        in_turn([direction(0, c, of_ref), direction(1, n_chunks - 1 - c, ob_ref)])
        return carry

    lax.cond(jnp.max(worst) <= FAST_LIMIT,
             lambda: lax.fori_loop(0, n_fast // unroll, fast_step, 0),
             lambda: lax.fori_loop(0, n_chunks, step, 0))

    def finish(c, carry):
        rows = pl.ds(pl.multiple_of(c * L, L), L)
        o = of_ref[rows, :] + ob_ref[rows, :]
        g = proj_ref[rows, g_cols]
        parts = [_rms(o[:, h * DV_R:(h + 1) * DV_R]) for h in range(H_R)]
        y = jnp.concatenate(parts, axis=1) * gn_ref[...]
        o_ref[rows, :] = y * (g / (1.0 + jnp.exp(-g)))
        return carry

    lax.fori_loop(0, n_chunks, finish, 0)

    if want_s:
        for d in range(2):
            for h in range(H_R):
                s_ref[d, h] = st_ref[d, h].T


def _hgrn(hb, w_in, lb_raw, gn, layer, batch, seq, s0=None, s_prev=None, want_s=False):
    has_s0 = s0 is not None
    state_spec = pl.BlockSpec((None, None, 2, H_R, DK_R, DV_R), lambda b: (b, layer, 0, 0, 0, 0))
    in_specs = [pl.BlockSpec((seq, D_MODEL), lambda b: (b, 0)),
                pl.BlockSpec((None, D_MODEL, IN_R), lambda b: (layer, 0, 0)),
                pl.BlockSpec((DEPTH, 2, W_R), lambda b: (0, 0, 0)),
                _layer_spec(gn, layer)]
    args = [hb, w_in, lb_raw, gn]
    if has_s0:
        in_specs.append(state_spec)
        args.append(s0)
    out_shape = [jax.ShapeDtypeStruct((batch * seq, W_R), F32)]
    out_specs = [pl.BlockSpec((seq, W_R), lambda b: (b, 0))]
    aliases = {}
    if want_s:
        out_shape.append(jax.ShapeDtypeStruct((batch, DEPTH, 2, H_R, DK_R, DV_R), F32))
        out_specs.append(state_spec)
        if s_prev is not None:
            aliases[len(args)] = 1
            in_specs.append(pl.BlockSpec(memory_space=pl.ANY))
            args.append(s_prev)
    res = pl.pallas_call(
        functools.partial(_hgrn_kernel, layer=layer, seq=seq, has_s0=has_s0, has_prev=bool(aliases),
                          want_s=want_s),
        out_shape=out_shape,
        grid=(batch,),
        in_specs=in_specs,
        out_specs=out_specs,
        scratch_shapes=[pltpu.VMEM((seq, IN_R), F32),
                        pltpu.VMEM((seq, W_R), F32), pltpu.VMEM((seq, W_R), F32),
                        pltpu.VMEM((2, H_R, DV_R, DK_R), F32),
                        pltpu.VMEM((2, HGRN_CHUNK, HGRN_CHUNK), jnp.int32),
                        pltpu.VMEM((2, seq, W_R), F32), pltpu.VMEM((2, seq, W_R), F32)],
        input_output_aliases=aliases,
        compiler_params=_params("arbitrary"),
        name="hgrn",
    )(*args)
    return res if want_s else (res[0], None)


def _head_mean_square(x):
    w = x.shape[1]
    shift = HEAD_DIM.bit_length() - 1
    r = lax.broadcasted_iota(jnp.int32, (w, w), 0) >> shift
    c = lax.broadcasted_iota(jnp.int32, (w, w), 1) >> shift
    ones = jnp.where(r == c, 1.0, 0.0).astype(BF16)
    return _dot((x * x).astype(BF16), ones) * (1.0 / HEAD_DIM)


def _attn_kernel(*refs, seq, past, n_prev):
    refs = list(refs)
    h_ref, wq0_ref, wq1_ref, wkv_ref, qg_ref, kg_ref = refs[:6]
    pos = 6
    if past:
        ck_ref, cv_ref, cos_ref, sa_ref, sb_ref = refs[pos:pos + 5]
        pos += 5
        o_ref = refs[pos]
        pos += 1
    else:
        pos += n_prev
        o_ref, kn_ref, vn_ref = refs[pos:pos + 3]
        pos += 3
    qs_ref, ks_ref, vt_ref = refs[pos:pos + 3]

    def rope(x):
        up = pltpu.roll(x, LANES - ROPE_AXIS_DIM // 2, axis=1)
        dn = pltpu.roll(x, ROPE_AXIS_DIM // 2, axis=1)
        return x * cos_ref[...] + up * sa_ref[...] + dn * sb_ref[...]

    hb = h_ref[...]
    q = jnp.concatenate([_dot(hb, wq0_ref[...]), _dot(hb, wq1_ref[...])], axis=1)
    kv = _dot(hb, wkv_ref[...])
    k = kv[:, 0:LANES]
    v = kv[:, LANES:2 * LANES]
    qn = q * lax.rsqrt(_head_mean_square(q) + EPS) * qg_ref[...]
    kn = k * lax.rsqrt(_head_mean_square(k) + EPS) * kg_ref[...]
    scale = HEAD_DIM ** -0.5 * LOG2E
    low = lax.broadcasted_iota(jnp.int32, (seq, LANES), 1) < HEAD_DIM
    pairs_per_kv = (W_A // LANES) // KV_A
    for p in range(W_A // LANES):
        cs = slice(p * LANES, (p + 1) * LANES)
        qp = (rope(qn[:, cs]) if past else qn[:, cs]) * scale
        even = jnp.where(low, qp, 0.0)
        odd = jnp.where(low, 0.0, qp)
        if p // pairs_per_kv == 0:
            odd = pltpu.roll(odd, HEAD_DIM, axis=1)
        else:
            even = pltpu.roll(even, HEAD_DIM, axis=1)
        qs_ref[0, :, cs] = even.astype(BF16)
        qs_ref[1, :, cs] = odd.astype(BF16)
    if past:
        ks_ref[...] = jnp.concatenate([ck_ref[...], rope(kn)], axis=0).astype(BF16)
        vt_ref[...] = jnp.concatenate([cv_ref[...], v], axis=0).T.astype(BF16)
    else:
        ks_ref[...] = kn.astype(BF16)
        vt_ref[...] = v.T.astype(BF16)
        kn_ref[...] = kn
        vn_ref[...] = v

    tq = ATTN_QROWS
    total = past + seq

    n_blocks = total // ATTN_KEYS
    steps = [(p, kc) for p in range(W_A // LANES) for kc in range(n_blocks)]

    def tile(r, carry):
        rows = pl.ds(pl.multiple_of(r * tq, tq), tq)

        def scores(step):
            p, kc = step
            cs = slice(p * LANES, (p + 1) * LANES)
            q2 = jnp.concatenate([qs_ref[0, rows, cs], qs_ref[1, rows, cs]], axis=0)
            return _dot_nt(ks_ref[kc * ATTN_KEYS:(kc + 1) * ATTN_KEYS, :], q2)

        ahead = [scores(st) for st in steps[:ATTN_LOOKAHEAD]]
        for i, (p, kc) in enumerate(steps):
            s = ahead.pop(0)
            if i + ATTN_LOOKAHEAD < len(steps):
                ahead.append(scores(steps[i + ATTN_LOOKAHEAD]))
            kh = p // pairs_per_kv
            v_t = vt_ref[kh * HEAD_DIM:(kh + 1) * HEAD_DIM, kc * ATTN_KEYS:(kc + 1) * ATTN_KEYS]
            s_max = jnp.max(s, axis=0, keepdims=True)
            if kc == 0:
                m = s_max
                pexp = jnp.exp2(s - m)
                den = jnp.sum(pexp, axis=0, keepdims=True)
                o2 = _dot(v_t, pexp.astype(BF16))
            else:
                m_new = jnp.maximum(m, s_max)
                alpha = jnp.exp2(m - m_new)
                pexp = jnp.exp2(s - m_new)
                den = den * alpha + jnp.sum(pexp, axis=0, keepdims=True)
                o2 = o2 * alpha + _dot(v_t, pexp.astype(BF16))
                m = m_new
            if kc == n_blocks - 1:
                o2 = o2 / den
                o_ref[rows, p * LANES:(p + 1) * LANES] = jnp.concatenate([o2[:, 0:tq], o2[:, tq:2 * tq]], axis=0).T
        return carry

    lax.fori_loop(0, seq // tq, tile, 0)


def _attn(hb, w_in, qg, kg, batch, seq, layer, cache=None, rope_tabs=None, kv_prev=None):
    past = 0 if cache is None else cache[0].shape[2]
    wide = 2 * LANES
    w_specs = [pl.BlockSpec((None, D_MODEL, wide), lambda b, j=j: (layer, 0, IN_R // wide + j))
               for j in range(IN_A // wide)]
    in_specs = [pl.BlockSpec((seq, D_MODEL), lambda b: (b, 0)), *w_specs,
                _layer_spec(qg, layer), _layer_spec(kg, layer)]
    args = [hb, w_in, w_in, w_in, qg, kg]
    out_shape = [jax.ShapeDtypeStruct((batch * seq, W_A), F32)]
    out_specs = [pl.BlockSpec((seq, W_A), lambda b: (b, 0))]
    aliases = {}
    if past:
        ck, cv = cache
        cspec = pl.BlockSpec((None, None, past, LANES), lambda b: (b, layer, 0, 0))
        tspec = pl.BlockSpec((seq, LANES), lambda b: (0, 0))
        in_specs += [cspec, cspec, tspec, tspec, tspec]
        args += [ck, cv, *rope_tabs]
    else:
        for i in range(2):
            out_shape.append(jax.ShapeDtypeStruct((batch, DEPTH, seq, LANES), F32))
            out_specs.append(pl.BlockSpec((None, None, seq, LANES), lambda b: (b, layer, 0, 0)))
            if kv_prev is not None:
                aliases[len(args)] = 1 + i
                in_specs.append(pl.BlockSpec(memory_space=pl.ANY))
                args.append(kv_prev[i])
    res = pl.pallas_call(
        functools.partial(_attn_kernel, seq=seq, past=past, n_prev=len(aliases)),
        out_shape=out_shape,
        grid=(batch,),
        in_specs=in_specs,
        out_specs=out_specs,
        scratch_shapes=[pltpu.VMEM((2, seq, W_A), BF16),
                        pltpu.VMEM((past + seq, LANES), BF16),
                        pltpu.VMEM((LANES, past + seq), BF16)],
        input_output_aliases=aliases,
        compiler_params=_params("arbitrary"),
        name="attn",
    )(*args)
    return res


def _outffn_kernel(*refs, final):
    refs = list(refs)
    x_ref, or_ref, oa_ref, mod_ref, g2_ref, wout_ref, w1_ref, w2_ref, last_ref = refs[:9]
    pos = 9
    if not final:
        nmod_ref = refs[pos]
        pos += 1
    o_ref = refs[pos]
    gate1 = mod_ref[:, 2 * D_MODEL:3 * D_MODEL]
    shift2 = mod_ref[:, 3 * D_MODEL:4 * D_MODEL]
    scale2 = mod_ref[:, 4 * D_MODEL:5 * D_MODEL]
    gate2 = mod_ref[:, 5 * D_MODEL:6 * D_MODEL]
    mix = _dot(or_ref[...].astype(BF16), wout_ref[0:W_R, :]) + _dot(oa_ref[...].astype(BF16), wout_ref[W_R:W_R + W_A, :])
    x1 = x_ref[...] + gate1 * mix
    h2 = _prenorm(x1, shift2, scale2, g2_ref[...])
    acc = jnp.zeros_like(x1)
    for j in range(D_FF // FFN_CHUNK):
        cs = slice(j * FFN_CHUNK, (j + 1) * FFN_CHUNK)
        hid = jnp.maximum(_dot(h2, w1_ref[:, cs]), 0.0)
        acc = acc + _dot((hid * hid).astype(BF16), w2_ref[cs, :])
    x2 = x1 + gate2 * acc
    if final:
        o_ref[...] = _rms(x2) * last_ref[...]
    else:
        o_ref[...] = x2
        hn_ref = refs[pos + 1]
        hn_ref[...] = _prenorm(x2, nmod_ref[:, 0:D_MODEL], nmod_ref[:, D_MODEL:2 * D_MODEL], last_ref[...])


def _outffn(x, o_r, o_a, mod, g1, g2, wout, w1, w2, gf, layer, rows_per_seq, ctx):
    n = x.shape[0]
    tm = ROW_TILE
    final = layer == DEPTH - 1
    row = lambda w: pl.BlockSpec((tm, w), lambda i: (i, 0))
    in_specs = [row(D_MODEL), row(W_R), row(W_A),
                _mod_spec(layer, rows_per_seq, tm, ctx),
                _layer_spec(g2, layer), _layer_spec(wout, layer), _layer_spec(w1, layer), _layer_spec(w2, layer)]
    args = [x, o_r, o_a, mod, g2, wout, w1, w2]
    out_shape = [jax.ShapeDtypeStruct((n, D_MODEL), F32)]
    out_specs = [row(D_MODEL)]
    if final:
        in_specs.append(pl.BlockSpec(gf.shape, lambda i: (0, 0)))
        args.append(gf)
    else:
        in_specs += [_layer_spec(g1, layer + 1), _mod_spec(layer + 1, rows_per_seq, tm, ctx)]
        args += [g1, mod]
        out_shape.append(jax.ShapeDtypeStruct((n, D_MODEL), BF16))
        out_specs.append(row(D_MODEL))
    res = pl.pallas_call(
        functools.partial(_outffn_kernel, final=final),
        out_shape=out_shape,
        grid=(n // tm,),
        in_specs=in_specs,
        out_specs=out_specs,
        compiler_params=_params("arbitrary"),
        name="outffn",
    )(*args)
    return (res[0], None) if final else res


def _rope_tables(seq):
    rows = seq // GRID_W
    rowi = jnp.repeat(jnp.arange(rows, dtype=F32), GRID_W)
    coli = jnp.tile(jnp.arange(GRID_W, dtype=F32), rows)
    inv = ROPE_THETA ** (-jnp.arange(0, ROPE_AXIS_DIM, 2, dtype=F32) / ROPE_AXIS_DIM)
    ar = rowi[:, None] * inv[None, :]
    ac = coli[:, None] * inv[None, :]
    ang = jnp.concatenate([ar, ar, ac, ac], axis=-1)
    cos = jnp.tile(jnp.cos(ang), (1, LANES // HEAD_DIM))
    sin = jnp.tile(jnp.sin(ang), (1, LANES // HEAD_DIM))
    first = (jnp.arange(LANES) % ROPE_AXIS_DIM) < ROPE_AXIS_DIM // 2
    return cos, jnp.where(first, -sin, 0.0), jnp.where(first, 0.0, sin)


def kernel(x_prompt, x_sample, cache_k, cache_v, state_hgrn, c, c_ctx, w_mod, b_mod, norm1_g, w_in, lb_raw,
           hgrn_norm_g, q_norm_g, k_norm_g, w_out, norm2_g, w1, w2, final_norm_g):
    bp, sp, _ = x_prompt.shape
    bs, ss, _ = x_sample.shape
    past = cache_k.shape[2]

    cvec = jnp.zeros((MOD_ROWS, D_MODEL), F32).at[0:bs].set(c).at[MOD_ROWS // 2].set(c_ctx)
    mod = _modulation(cvec, w_mod, b_mod).reshape(DEPTH, MOD_ROWS, 1, MOD_W)

    w_in_b, w_out_b, w1_b, w2_b = (w.astype(BF16) for w in (w_in, w_out, w1, w2))
    ck = cache_k.reshape(bs, DEPTH, past, KV_A * HEAD_DIM)
    cv = cache_v.reshape(bs, DEPTH, past, KV_A * HEAD_DIM)
    tabs = _rope_tables(ss)
    gf = final_norm_g.reshape(1, D_MODEL)
    g1 = norm1_g.reshape(DEPTH, 1, D_MODEL)
    g2 = norm2_g.reshape(DEPTH, 1, D_MODEL)
    gn = jnp.tile(hgrn_norm_g, (1, H_R)).reshape(DEPTH, 1, W_R)
    qg = jnp.tile(q_norm_g, (1, H_A)).reshape(DEPTH, 1, W_A)
    kg = jnp.tile(k_norm_g, (1, KV_A)).reshape(DEPTH, 1, LANES)

    xp = x_prompt.reshape(bp * sp, D_MODEL)
    xs = x_sample.reshape(bs * ss, D_MODEL)
    hp = _first_prenorm(xp, mod, g1, sp, True)
    hs = _first_prenorm(xs, mod, g1, ss, False)
    new_kv, new_s = None, None
    for l in range(DEPTH):
        o_r, new_s = _hgrn(hp, w_in_b, lb_raw, gn, l, bp, sp, s_prev=new_s, want_s=True)
        o_a, *new_kv = _attn(hp, w_in_b, qg, kg, bp, sp, l, kv_prev=new_kv)
        xp, hp = _outffn(xp, o_r, o_a, mod, g1, g2, w_out_b, w1_b, w2_b, gf, l, sp, True)

        o_r, _ = _hgrn(hs, w_in_b, lb_raw, gn, l, bs, ss, s0=state_hgrn)
        o_a, = _attn(hs, w_in_b, qg, kg, bs, ss, l, cache=(ck, cv), rope_tabs=tabs)
        xs, hs = _outffn(xs, o_r, o_a, mod, g1, g2, w_out_b, w1_b, w2_b, gf, l, ss, False)

    y_prompt = xp.reshape(bp, sp, D_MODEL)
    y_sample = xs.reshape(bs, ss, D_MODEL)
    new_k = new_kv[0].reshape(bp, DEPTH, sp, KV_A, HEAD_DIM)
    new_v = new_kv[1].reshape(bp, DEPTH, sp, KV_A, HEAD_DIM)
    return (y_prompt, y_sample, new_k, new_v, new_s)
```

```python
import functools

import jax
import jax.numpy as jnp
import numpy as np
from jax import lax
from jax.experimental import pallas as pl
from jax.experimental.pallas import tpu as pltpu

F32 = jnp.float32
BF16 = jnp.bfloat16

D_MODEL = 1024
DEPTH = 4
GRID_W = 64
D_FF = 4 * D_MODEL
H_R = 4
DK_R = 128
DV_R = 128
W_R = H_R * DV_R
H_A = 8
KV_A = 2
HEAD_DIM = 64
W_A = H_A * HEAD_DIM
ROPE_AXIS_DIM = HEAD_DIM // 2
ROPE_THETA = 10000.0
EPS = 1e-6
LOG2E = 1.4426950408889634
IN_R = 3 * H_R * DK_R + 2 * W_R
IN_A = W_A + 2 * KV_A * HEAD_DIM
MOD_W = 6 * D_MODEL
MOD_ROWS = 8

LANES = 128
SUBLANES = 8
VMEM_LIMIT = 56 * 1024 * 1024

HGRN_CHUNK = 128
FAST_CHUNK = 64
FAST_UNROLL = 8
FAST_LIMIT = 115.0
PROJ_ROWS = 256
HGRN_ROWS = 512
ATTN_QROWS = 256
ATTN_KEYS = 256
ATTN_ROWS = 1024
ATTN_LOOKAHEAD = 2
FFN_CHUNK = 1024
ROW_TILE = 512


def _dot(a, b):
    return jnp.dot(a, b, preferred_element_type=F32)


def _dot_nt(a, b):
    return lax.dot_general(a, b, (((1,), (1,)), ((), ())), preferred_element_type=F32)


def _params(*sem):
    return pltpu.CompilerParams(dimension_semantics=sem, vmem_limit_bytes=VMEM_LIMIT)


def _rms(x):
    return x * lax.rsqrt(jnp.mean(x * x, axis=-1, keepdims=True) + EPS)


def _mod_spec(layer, rows_per_seq, tile, ctx):
    per = rows_per_seq // tile
    row = (lambda i: MOD_ROWS // 2) if ctx else (lambda i: i // per)
    return pl.BlockSpec((None, None, 1, MOD_W), lambda i: (layer, row(i), 0, 0))


def _layer_spec(a, layer):
    return pl.BlockSpec((None,) + a.shape[1:], lambda i: (layer, 0, 0))


def _prenorm(x, shift, scale, g):
    return (_rms(x) * g * (1.0 + scale) + shift).astype(BF16)


def _mod_kernel(c_ref, w_ref, b_ref, o_ref):
    c = c_ref[...]
    s = c / (1.0 + jnp.exp(-c))
    o_ref[...] = _dot(s.astype(BF16), w_ref[...].astype(BF16)) + b_ref[...]


def _modulation(cvec, w_mod, b_mod):
    tn = 1536
    return pl.pallas_call(
        _mod_kernel,
        out_shape=jax.ShapeDtypeStruct((DEPTH, MOD_ROWS, MOD_W), F32),
        grid=(DEPTH, MOD_W // tn),
        in_specs=[
            pl.BlockSpec((MOD_ROWS, D_MODEL), lambda l, j: (0, 0)),
            pl.BlockSpec((None, D_MODEL, tn), lambda l, j: (l, 0, j)),
            pl.BlockSpec((None, 1, tn), lambda l, j: (l, 0, j)),
        ],
        out_specs=pl.BlockSpec((None, MOD_ROWS, tn), lambda l, j: (l, 0, j)),
        compiler_params=_params("arbitrary", "arbitrary"),
        name="modulation",
    )(cvec, w_mod, b_mod.reshape(DEPTH, 1, MOD_W))


def _prenorm_kernel(x_ref, mod_ref, g_ref, o_ref):
    o_ref[...] = _prenorm(x_ref[...], mod_ref[:, 0:D_MODEL], mod_ref[:, D_MODEL:2 * D_MODEL], g_ref[...])


def _first_prenorm(x, mod, g1, rows_per_seq, ctx):
    n = x.shape[0]
    tm = ROW_TILE
    return pl.pallas_call(
        _prenorm_kernel,
        out_shape=jax.ShapeDtypeStruct((n, D_MODEL), BF16),
        grid=(n // tm,),
        in_specs=[pl.BlockSpec((tm, D_MODEL), lambda i: (i, 0)),
                  _mod_spec(0, rows_per_seq, tm, ctx),
                  _layer_spec(g1, 0)],
        out_specs=pl.BlockSpec((tm, D_MODEL), lambda i: (i, 0)),
        compiler_params=_params("arbitrary"),
        name="prenorm",
    )(x, mod, g1)


def _bcast_block_row(b, blk, r):
    n, w = b.shape
    parts = [jnp.broadcast_to(b[k * blk + r:k * blk + r + 1, :], (blk, w)) for k in range(n // blk)]
    return parts[0] if len(parts) == 1 else jnp.concatenate(parts, axis=0)


def _hgrn_kernel(*refs, layer, seq, nseq, has_s0, has_prev, want_s):
    L = HGRN_CHUNK
    n_chunks = seq // L
    n_levels = int(np.log2(L))
    refs = list(refs)
    h_ref, w_ref, lbraw_ref, gn_ref = refs[:4]
    pos = 4
    s0_ref = None
    if has_s0:
        s0_ref = refs[pos]
        pos += 1
    if has_prev:
        pos += 1
    o_ref = refs[pos]
    pos += 1
    s_ref = None
    if want_s:
        s_ref = refs[pos]
        pos += 1
    proj_ref, of_ref, ob_ref, st_ref, code_ref, kg_ref, b_ref = refs[pos:pos + 7]
    q_cols, i_cols, g_cols = (slice(n * W_R, (n + 1) * W_R) for n in (0, 3, 4))
    f_cols = [slice(W_R, 2 * W_R), slice(2 * W_R, 3 * W_R)]

    def project(g):
        rows = slice(g * PROJ_ROWS, (g + 1) * PROJ_ROWS)
        proj_ref[rows, :] = _dot(h_ref[rows, :], w_ref[...])

    if layer > 0:
        raw = lbraw_ref[...]
        e = jnp.exp(raw - jnp.max(raw, axis=0, keepdims=True))
        p = e / jnp.sum(e, axis=0, keepdims=True)
        lb = p[1]
        for j in range(2, layer + 1):
            lb = lb + p[j]
        log_lb = jnp.log(lb)
        log_1m_lb = jnp.log1p(-lb)
        one_m_lb = 1.0 - lb

    def gates(fx, d):
        ls = jnp.minimum(fx, 0.0) - jnp.log(1.0 + jnp.exp(-jnp.abs(fx)))
        k = jnp.exp(ls - fx)
        if layer == 0:
            return ls, k
        a = log_lb[d:d + 1, :]
        c = log_1m_lb[d:d + 1, :] + ls
        lf = jnp.maximum(a, c) + jnp.log(1.0 + jnp.exp(-jnp.abs(a - c)))
        return lf, one_m_lb[d:d + 1, :] * k

    ti = lax.broadcasted_iota(jnp.int32, (L, L), 0)
    si = lax.broadcasted_iota(jnp.int32, (L, L), 1)
    x = ti ^ si
    lv = jnp.zeros((L, L), jnp.int32)
    for j in range(n_levels):
        lv = lv + jnp.where((x >> j) != 0, 1, 0)
    code_ref[0] = jnp.where(ti >= si, lv, -1)
    code_ref[1] = jnp.where(ti <= si, lv, -1)

    def init_states():
        for s in range(nseq):
            for d in range(2):
                for h in range(H_R):
                    if has_s0:
                        st_ref[s, d, h] = s0_ref[s, d, h].T
                    else:
                        st_ref[s, d, h] = jnp.zeros((DV_R, DK_R), F32)

    init_states()

    def cumulative(lf, tri):
        hi = lf.astype(BF16)
        r1 = lf - hi.astype(F32)
        mid = r1.astype(BF16)
        lo = (r1 - mid.astype(F32)).astype(BF16)
        b3 = _dot(tri, jnp.concatenate([hi, mid, lo], axis=1))
        return b3[:, 0:W_R] + b3[:, W_R:2 * W_R] + b3[:, 2 * W_R:3 * W_R]

    Lf = FAST_CHUNK
    n_fast = seq // Lf
    unroll = min(FAST_UNROLL, n_fast)
    ref_row = [Lf // 2 - 1, Lf // 2]
    first_row = [0, Lf - 1]
    last_row = [Lf - 1, 0]
    tf = lax.broadcasted_iota(jnp.int32, (Lf, Lf), 0)
    sf = lax.broadcasted_iota(jnp.int32, (Lf, Lf), 1)
    keep = [tf >= sf, tf <= sf]

    def prepare(g, worst):
        for i in range(PROJ_ROWS // Lf):
            rows = slice(g * PROJ_ROWS + i * Lf, g * PROJ_ROWS + (i + 1) * Lf)
            q_top = jnp.max(jnp.abs(proj_ref[rows, q_cols]), axis=0, keepdims=True)
            q_bits = jnp.log(jnp.maximum(q_top, 1.0)) * LOG2E
            for d in range(2):
                lf, k = gates(proj_ref[rows, f_cols[d]], d)
                b = cumulative(lf * LOG2E, jnp.where(keep[d], 1.0, 0.0).astype(BF16))
                kg_ref[d, rows, :] = k
                b_ref[d, rows, :] = b
                b_mid = b[ref_row[d]:ref_row[d] + 1, :]
                grow = jnp.maximum(b[first_row[d]:first_row[d] + 1, :] - b_mid,
                                   b_mid - b[last_row[d]:last_row[d] + 1, :])
                worst = jnp.maximum(worst, grow + q_bits)
        return worst

    def fast_direction(d, s, c, out_ref):
        start = s * seq + c * Lf
        rows = slice(start, start + Lf) if isinstance(start, int) else pl.ds(pl.multiple_of(start, Lf), Lf)
        q = proj_ref[rows, q_cols]
        v = proj_ref[rows, i_cols]
        k = kg_ref[d, rows, :]
        b = b_ref[d, rows, :]
        b_mid = b[ref_row[d]:ref_row[d] + 1, :]
        edge = b[last_row[d]:last_row[d] + 1, :]
        q_t = (q * jnp.exp2(b - b_mid)).astype(BF16)
        k_f = k * jnp.exp2(b_mid - b)
        k_t = k_f.astype(BF16)
        k_out = (k_f * jnp.exp2(edge - b_mid)).astype(BF16)
        dec = jnp.exp2(edge)
        from_start = jnp.exp2(b_mid)
        vb = v.astype(BF16)
        heads = [slice(h * DK_R, (h + 1) * DK_R) for h in range(H_R)]
        both = []
        for h, cs in enumerate(heads):
            st = st_ref[s, d, h]
            rhs = jnp.concatenate([(st * from_start[:, cs]).astype(BF16), k_t[:, cs]], axis=0)
            both.append(_dot_nt(q_t[:, cs], rhs))
            st_ref[s, d, h] = st * dec[:, cs] + _dot(v[:, cs].T.astype(BF16), k_out[:, cs])
        yield
        outs = []
        for h, cs in enumerate(heads):
            a = jnp.where(keep[d], both[h][:, DV_R:DV_R + Lf], 0.0).astype(BF16)
            outs.append(both[h][:, 0:DV_R] + _dot(a, vb[:, cs]))
        out_ref[rows, :] = jnp.concatenate(outs, axis=1)

    def in_turn(gens):
        active = list(gens)
        while active:
            for gen in list(active):
                if next(gen, "done") == "done":
                    active.remove(gen)

    def fast_chunks(s, first):
        gens = []
        for i in range(unroll):
            gens.append(fast_direction(0, s, first + i, of_ref))
            gens.append(fast_direction(1, s, n_fast - 1 - (first + i), ob_ref))
        in_turn(gens)

    n_groups = nseq * seq // PROJ_ROWS
    inline = n_fast == unroll
    worst = jnp.zeros((1, W_R), F32)
    project(0)
    for g in range(n_groups):
        if g + 1 < n_groups:
            project(g + 1)
        worst = prepare(g, worst)
        if inline and ((g + 1) * PROJ_ROWS) % seq == 0:
            fast_chunks((g + 1) * PROJ_ROWS // seq - 1, 0)
    if not inline:
        for s in range(nseq):
            def fast_step(c, carry, s=s):
                fast_chunks(s, c * unroll)
                return carry
            lax.fori_loop(0, n_fast // unroll, fast_step, 0)

    def direction(d, s, c, out_ref):
        rows = pl.ds(pl.multiple_of(s * seq + c * L, L), L)
        row = lax.broadcasted_iota(jnp.int32, (L, W_R), 0)
        code = code_ref[d]
        tri = jnp.where(code >= 0, 1.0, 0.0).astype(BF16)
        q = proj_ref[rows, q_cols]
        v = proj_ref[rows, i_cols]
        lf, k = gates(proj_ref[rows, f_cols[d]], d)
        lf = lf * LOG2E
        b = cumulative(lf, tri)
        yield

        vb = v.astype(BF16)
        a_h = [None] * H_R
        for j in range(1, n_levels + 1):
            half = 1 << (j - 1)
            blk = 2 * half
            u = row & (blk - 1)
            qside = (u >= half) if d == 0 else (u < half)
            if j == 1:
                e_j = jnp.where(qside, lf, 0.0)
            elif j == 2:
                up = pltpu.roll(lf, L - 1, axis=0)
                dn = pltpu.roll(lf, 1, axis=0)
                if d == 0:
                    e_j = jnp.where(u == 0, up, jnp.where(u == 1, 0.0, jnp.where(u == 2, lf, lf + dn)))
                else:
                    e_j = jnp.where(u == 0, lf + up, jnp.where(u == 1, lf, jnp.where(u == 2, 0.0, dn)))
            else:
                diff = b - _bcast_block_row(b, blk, half - 1 + d)
                e_j = jnp.where(qside, diff, -diff)
            z = (jnp.where(qside, q, k) * jnp.exp2(e_j)).astype(BF16)
            for h in range(H_R):
                cs = slice(h * DK_R, (h + 1) * DK_R)
                p_j = _dot_nt(z[:, cs], z[:, cs])
                a_h[h] = jnp.where(code == j, p_j, 0.0 if a_h[h] is None else a_h[h])
            yield

        edge = b[L - 1:L, :] if d == 0 else b[0:1, :]
        q_in = (q * jnp.exp2(b)).astype(BF16)
        k_out = (k * jnp.exp2(edge - b)).astype(BF16)
        dec = jnp.exp2(edge)
        qk = q * k
        outs = []
        for h in range(H_R):
            cs = slice(h * DK_R, (h + 1) * DK_R)
            st = st_ref[s, d, h]
            o = _dot(a_h[h].astype(BF16), vb[:, cs]) + _dot_nt(q_in[:, cs], st.astype(BF16))
            o = o + jnp.sum(qk[:, cs], axis=-1, keepdims=True) * v[:, cs]
            st_ref[s, d, h] = st * dec[:, cs] + _dot(v[:, cs].T.astype(BF16), k_out[:, cs])
            outs.append(o)
        out_ref[rows, :] = jnp.concatenate(outs, axis=1)

    def step(i, carry):
        s, c = i // n_chunks, i % n_chunks
        in_turn([direction(0, s, c, of_ref), direction(1, s, n_chunks - 1 - c, ob_ref)])
        return carry

    @pl.when(jnp.max(worst) > FAST_LIMIT)
    def _():
        init_states()
        lax.fori_loop(0, nseq * n_chunks, step, 0)

    def finish(c, carry):
        rows = pl.ds(pl.multiple_of(c * L, L), L)
        o = of_ref[rows, :] + ob_ref[rows, :]
        g = proj_ref[rows, g_cols]
        parts = [_rms(o[:, h * DV_R:(h + 1) * DV_R]) for h in range(H_R)]
        y = jnp.concatenate(parts, axis=1) * gn_ref[...]
        o_ref[rows, :] = y * (g / (1.0 + jnp.exp(-g)))
        return carry

    lax.fori_loop(0, nseq * n_chunks, finish, 0)

    if want_s:
        for s in range(nseq):
            for d in range(2):
                for h in range(H_R):
                    s_ref[s, d, h] = st_ref[s, d, h].T


def _hgrn(hb, w_in, lb_raw, gn, layer, batch, seq, s0=None, s_prev=None, want_s=False):
    has_s0 = s0 is not None
    nseq = max(1, HGRN_ROWS // seq)
    rows = nseq * seq
    state_spec = pl.BlockSpec((nseq, None, 2, H_R, DK_R, DV_R), lambda b: (b, layer, 0, 0, 0, 0))
    in_specs = [pl.BlockSpec((rows, D_MODEL), lambda b: (b, 0)),
                pl.BlockSpec((None, D_MODEL, IN_R), lambda b: (layer, 0, 0)),
                pl.BlockSpec((DEPTH, 2, W_R), lambda b: (0, 0, 0)),
                _layer_spec(gn, layer)]
    args = [hb, w_in, lb_raw, gn]
    if has_s0:
        in_specs.append(state_spec)
        args.append(s0)
    out_shape = [jax.ShapeDtypeStruct((batch * seq, W_R), F32)]
    out_specs = [pl.BlockSpec((rows, W_R), lambda b: (b, 0))]
    aliases = {}
    if want_s:
        out_shape.append(jax.ShapeDtypeStruct((batch, DEPTH, 2, H_R, DK_R, DV_R), F32))
        out_specs.append(state_spec)
        if s_prev is not None:
            aliases[len(args)] = 1
            in_specs.append(pl.BlockSpec(memory_space=pl.ANY))
            args.append(s_prev)
    res = pl.pallas_call(
        functools.partial(_hgrn_kernel, layer=layer, seq=seq, nseq=nseq, has_s0=has_s0,
                          has_prev=bool(aliases), want_s=want_s),
        out_shape=out_shape,
        grid=(batch // nseq,),
        in_specs=in_specs,
        out_specs=out_specs,
        scratch_shapes=[pltpu.VMEM((rows, IN_R), F32),
                        pltpu.VMEM((rows, W_R), F32), pltpu.VMEM((rows, W_R), F32),
                        pltpu.VMEM((nseq, 2, H_R, DV_R, DK_R), F32),
                        pltpu.VMEM((2, HGRN_CHUNK, HGRN_CHUNK), jnp.int32),
                        pltpu.VMEM((2, rows, W_R), F32), pltpu.VMEM((2, rows, W_R), F32)],
        input_output_aliases=aliases,
        compiler_params=_params("arbitrary"),
        name="hgrn",
    )(*args)
    return res if want_s else (res[0], None)


def _head_mean_square(x):
    w = x.shape[1]
    shift = HEAD_DIM.bit_length() - 1
    r = lax.broadcasted_iota(jnp.int32, (w, w), 0) >> shift
    c = lax.broadcasted_iota(jnp.int32, (w, w), 1) >> shift
    ones = jnp.where(r == c, 1.0, 0.0).astype(BF16)
    return _dot((x * x).astype(BF16), ones) * (1.0 / HEAD_DIM)


def _attn_kernel(*refs, seq, nseq, past, n_prev):
    refs = list(refs)
    h_ref, wq0_ref, wq1_ref, wkv_ref, qg_ref, kg_ref = refs[:6]
    pos = 6
    if past:
        ck_ref, cv_ref, cos_ref, sa_ref, sb_ref = refs[pos:pos + 5]
        pos += 5
        o_ref = refs[pos]
        pos += 1
    else:
        pos += n_prev
        o_ref, kn_ref, vn_ref = refs[pos:pos + 3]
        pos += 3
    qs_ref, ks_ref, vt_ref = refs[pos:pos + 3]

    def rope(x):
        up = pltpu.roll(x, LANES - ROPE_AXIS_DIM // 2, axis=1)
        dn = pltpu.roll(x, ROPE_AXIS_DIM // 2, axis=1)
        return x * cos_ref[...] + up * sa_ref[...] + dn * sb_ref[...]

    hb = h_ref[...]
    q = jnp.concatenate([_dot(hb, wq0_ref[...]), _dot(hb, wq1_ref[...])], axis=1)
    kv = _dot(hb, wkv_ref[...])
    k = kv[:, 0:LANES]
    v = kv[:, LANES:2 * LANES]
    qn = q * lax.rsqrt(_head_mean_square(q) + EPS) * qg_ref[...]
    kn = k * lax.rsqrt(_head_mean_square(k) + EPS) * kg_ref[...]
    scale = HEAD_DIM ** -0.5 * LOG2E
    low = lax.broadcasted_iota(jnp.int32, (nseq * seq, LANES), 1) < HEAD_DIM
    pairs_per_kv = (W_A // LANES) // KV_A
    for p in range(W_A // LANES):
        cs = slice(p * LANES, (p + 1) * LANES)
        qp = (rope(qn[:, cs]) if past else qn[:, cs]) * scale
        even = jnp.where(low, qp, 0.0)
        odd = jnp.where(low, 0.0, qp)
        if p // pairs_per_kv == 0:
            odd = pltpu.roll(odd, HEAD_DIM, axis=1)
        else:
            even = pltpu.roll(even, HEAD_DIM, axis=1)
        qs_ref[0, :, cs] = even.astype(BF16)
        qs_ref[1, :, cs] = odd.astype(BF16)
    if past:
        ks_ref[0] = jnp.concatenate([ck_ref[...], rope(kn)], axis=0).astype(BF16)
        vt_ref[0] = jnp.concatenate([cv_ref[...], v], axis=0).T.astype(BF16)
    else:
        for i in range(nseq):
            rs = slice(i * seq, (i + 1) * seq)
            ks_ref[i] = kn[rs, :].astype(BF16)
            vt_ref[i] = v[rs, :].T.astype(BF16)
            kn_ref[i] = kn[rs, :]
            vn_ref[i] = v[rs, :]

    tq = min(ATTN_QROWS, seq)
    total = past + seq

    kb = ATTN_KEYS if total % ATTN_KEYS == 0 else total
    n_blocks = total // kb
    n_tiles = seq // tq
    seqs = range(nseq) if n_tiles == 1 else range(1)
    steps = [(i, p, kc) for i in seqs for p in range(W_A // LANES) for kc in range(n_blocks)]

    def tile(r, carry):

        def q_rows(i):
            start = i * seq + r * tq
            return slice(start, start + tq) if isinstance(start, int) else pl.ds(pl.multiple_of(start, tq), tq)

        def scores(step):
            i, p, kc = step
            cs = slice(p * LANES, (p + 1) * LANES)
            q2 = jnp.concatenate([qs_ref[0, q_rows(i), cs], qs_ref[1, q_rows(i), cs]], axis=0)
            return _dot_nt(ks_ref[i, kc * kb:(kc + 1) * kb, :], q2)

        ahead = [scores(st) for st in steps[:ATTN_LOOKAHEAD]]
        for n, (i, p, kc) in enumerate(steps):
            s = ahead.pop(0)
            if n + ATTN_LOOKAHEAD < len(steps):
                ahead.append(scores(steps[n + ATTN_LOOKAHEAD]))
            kh = p // pairs_per_kv
            v_t = vt_ref[i, kh * HEAD_DIM:(kh + 1) * HEAD_DIM, kc * kb:(kc + 1) * kb]
            s_max = jnp.max(s, axis=0, keepdims=True)
            if kc == 0:
                m = s_max
                pexp = jnp.exp2(s - m)
                den = jnp.sum(pexp, axis=0, keepdims=True)
                o2 = _dot(v_t, pexp.astype(BF16))
            else:
                m_new = jnp.maximum(m, s_max)
                alpha = jnp.exp2(m - m_new)
                pexp = jnp.exp2(s - m_new)
                den = den * alpha + jnp.sum(pexp, axis=0, keepdims=True)
                o2 = o2 * alpha + _dot(v_t, pexp.astype(BF16))
                m = m_new
            if kc == n_blocks - 1:
                o2 = o2 / den
                o_ref[q_rows(i), p * LANES:(p + 1) * LANES] = jnp.concatenate([o2[:, 0:tq], o2[:, tq:2 * tq]], axis=0).T
        return carry

    if n_tiles == 1:
        tile(0, 0)
    else:
        assert nseq == 1
        lax.fori_loop(0, n_tiles, tile, 0)


def _attn(hb, w_in, qg, kg, batch, seq, layer, cache=None, rope_tabs=None, kv_prev=None):
    past = 0 if cache is None else cache[0].shape[2]
    wide = 2 * LANES
    w_specs = [pl.BlockSpec((None, D_MODEL, wide), lambda b, j=j: (layer, 0, IN_R // wide + j))
               for j in range(IN_A // wide)]
    nseq = max(1, ATTN_ROWS // seq)
    rows = nseq * seq
    in_specs = [pl.BlockSpec((rows, D_MODEL), lambda b: (b, 0)), *w_specs,
                _layer_spec(qg, layer), _layer_spec(kg, layer)]
    args = [hb, w_in, w_in, w_in, qg, kg]
    out_shape = [jax.ShapeDtypeStruct((batch * seq, W_A), F32)]
    out_specs = [pl.BlockSpec((rows, W_A), lambda b: (b, 0))]
    aliases = {}
    if past:
        ck, cv = cache
        cspec = pl.BlockSpec((None, None, past, LANES), lambda b: (b, layer, 0, 0))
        tspec = pl.BlockSpec((seq, LANES), lambda b: (0, 0))
        in_specs += [cspec, cspec, tspec, tspec, tspec]
        args += [ck, cv, *rope_tabs]
    else:
        for i in range(2):
            out_shape.append(jax.ShapeDtypeStruct((batch, DEPTH, seq, LANES), F32))
            out_specs.append(pl.BlockSpec((nseq, None, seq, LANES), lambda b: (b, layer, 0, 0)))
            if kv_prev is not None:
                aliases[len(args)] = 1 + i
                in_specs.append(pl.BlockSpec(memory_space=pl.ANY))
                args.append(kv_prev[i])
    res = pl.pallas_call(
        functools.partial(_attn_kernel, seq=seq, nseq=nseq, past=past, n_prev=len(aliases)),
        out_shape=out_shape,
        grid=(batch // nseq,),
        in_specs=in_specs,
        out_specs=out_specs,
        scratch_shapes=[pltpu.VMEM((2, rows, W_A), BF16),
                        pltpu.VMEM((nseq, past + seq, LANES), BF16),
                        pltpu.VMEM((nseq, LANES, past + seq), BF16)],
        input_output_aliases=aliases,
        compiler_params=_params("arbitrary"),
        name="attn",
    )(*args)
    return res


def _outffn_kernel(*refs, final):
    refs = list(refs)
    x_ref, or_ref, oa_ref, mod_ref, g2_ref, wout_ref, w1_ref, w2_ref, last_ref = refs[:9]
    pos = 9
    if not final:
        nmod_ref = refs[pos]
        pos += 1
    o_ref = refs[pos]
    gate1 = mod_ref[:, 2 * D_MODEL:3 * D_MODEL]
    shift2 = mod_ref[:, 3 * D_MODEL:4 * D_MODEL]
    scale2 = mod_ref[:, 4 * D_MODEL:5 * D_MODEL]
    gate2 = mod_ref[:, 5 * D_MODEL:6 * D_MODEL]
    mix = _dot(or_ref[...].astype(BF16), wout_ref[0:W_R, :]) + _dot(oa_ref[...].astype(BF16), wout_ref[W_R:W_R + W_A, :])
    x1 = x_ref[...] + gate1 * mix
    h2 = _prenorm(x1, shift2, scale2, g2_ref[...])
    acc = jnp.zeros_like(x1)
    for j in range(D_FF // FFN_CHUNK):
        cs = slice(j * FFN_CHUNK, (j + 1) * FFN_CHUNK)
        hid = jnp.maximum(_dot(h2, w1_ref[:, cs]), 0.0)
        acc = acc + _dot((hid * hid).astype(BF16), w2_ref[cs, :])
    x2 = x1 + gate2 * acc
    if final:
        o_ref[...] = _rms(x2) * last_ref[...]
    else:
        o_ref[...] = x2
        hn_ref = refs[pos + 1]
        hn_ref[...] = _prenorm(x2, nmod_ref[:, 0:D_MODEL], nmod_ref[:, D_MODEL:2 * D_MODEL], last_ref[...])


def _outffn(x, o_r, o_a, mod, g1, g2, wout, w1, w2, gf, layer, rows_per_seq, ctx):
    n = x.shape[0]
    tm = ROW_TILE
    final = layer == DEPTH - 1
    row = lambda w: pl.BlockSpec((tm, w), lambda i: (i, 0))
    in_specs = [row(D_MODEL), row(W_R), row(W_A),
                _mod_spec(layer, rows_per_seq, tm, ctx),
                _layer_spec(g2, layer), _layer_spec(wout, layer), _layer_spec(w1, layer), _layer_spec(w2, layer)]
    args = [x, o_r, o_a, mod, g2, wout, w1, w2]
    out_shape = [jax.ShapeDtypeStruct((n, D_MODEL), F32)]
    out_specs = [row(D_MODEL)]
    if final:
        in_specs.append(pl.BlockSpec(gf.shape, lambda i: (0, 0)))
        args.append(gf)
    else:
        in_specs += [_layer_spec(g1, layer + 1), _mod_spec(layer + 1, rows_per_seq, tm, ctx)]
        args += [g1, mod]
        out_shape.append(jax.ShapeDtypeStruct((n, D_MODEL), BF16))
        out_specs.append(row(D_MODEL))
    res = pl.pallas_call(
        functools.partial(_outffn_kernel, final=final),
        out_shape=out_shape,
        grid=(n // tm,),
        in_specs=in_specs,
        out_specs=out_specs,
        compiler_params=_params("arbitrary"),
        name="outffn",
    )(*args)
    return (res[0], None) if final else res


def _rope_tables(seq):
    rows = seq // GRID_W
    rowi = jnp.repeat(jnp.arange(rows, dtype=F32), GRID_W)
    coli = jnp.tile(jnp.arange(GRID_W, dtype=F32), rows)
    inv = ROPE_THETA ** (-jnp.arange(0, ROPE_AXIS_DIM, 2, dtype=F32) / ROPE_AXIS_DIM)
    ar = rowi[:, None] * inv[None, :]
    ac = coli[:, None] * inv[None, :]
    ang = jnp.concatenate([ar, ar, ac, ac], axis=-1)
    cos = jnp.tile(jnp.cos(ang), (1, LANES // HEAD_DIM))
    sin = jnp.tile(jnp.sin(ang), (1, LANES // HEAD_DIM))
    first = (jnp.arange(LANES) % ROPE_AXIS_DIM) < ROPE_AXIS_DIM // 2
    return cos, jnp.where(first, -sin, 0.0), jnp.where(first, 0.0, sin)


def kernel(x_prompt, x_sample, cache_k, cache_v, state_hgrn, c, c_ctx, w_mod, b_mod, norm1_g, w_in, lb_raw,
           hgrn_norm_g, q_norm_g, k_norm_g, w_out, norm2_g, w1, w2, final_norm_g):
    bp, sp, _ = x_prompt.shape
    bs, ss, _ = x_sample.shape
    past = cache_k.shape[2]

    cvec = jnp.zeros((MOD_ROWS, D_MODEL), F32).at[0:bs].set(c).at[MOD_ROWS // 2].set(c_ctx)
    mod = _modulation(cvec, w_mod, b_mod).reshape(DEPTH, MOD_ROWS, 1, MOD_W)

    w_in_b, w_out_b, w1_b, w2_b = (w.astype(BF16) for w in (w_in, w_out, w1, w2))
    ck = cache_k.reshape(bs, DEPTH, past, KV_A * HEAD_DIM)
    cv = cache_v.reshape(bs, DEPTH, past, KV_A * HEAD_DIM)
    tabs = _rope_tables(ss)
    gf = final_norm_g.reshape(1, D_MODEL)
    g1 = norm1_g.reshape(DEPTH, 1, D_MODEL)
    g2 = norm2_g.reshape(DEPTH, 1, D_MODEL)
    gn = jnp.tile(hgrn_norm_g, (1, H_R)).reshape(DEPTH, 1, W_R)
    qg = jnp.tile(q_norm_g, (1, H_A)).reshape(DEPTH, 1, W_A)
    kg = jnp.tile(k_norm_g, (1, KV_A)).reshape(DEPTH, 1, LANES)

    xp = x_prompt.reshape(bp * sp, D_MODEL)
    xs = x_sample.reshape(bs * ss, D_MODEL)
    hp = _first_prenorm(xp, mod, g1, sp, True)
    hs = _first_prenorm(xs, mod, g1, ss, False)
    new_kv, new_s = None, None
    for l in range(DEPTH):
        o_r, new_s = _hgrn(hp, w_in_b, lb_raw, gn, l, bp, sp, s_prev=new_s, want_s=True)
        o_a, *new_kv = _attn(hp, w_in_b, qg, kg, bp, sp, l, kv_prev=new_kv)
        xp, hp = _outffn(xp, o_r, o_a, mod, g1, g2, w_out_b, w1_b, w2_b, gf, l, sp, True)

        o_r, _ = _hgrn(hs, w_in_b, lb_raw, gn, l, bs, ss, s0=state_hgrn)
        o_a, = _attn(hs, w_in_b, qg, kg, bs, ss, l, cache=(ck, cv), rope_tabs=tabs)
        xs, hs = _outffn(xs, o_r, o_a, mod, g1, g2, w_out_b, w1_b, w2_b, gf, l, ss, False)

    y_prompt = xp.reshape(bp, sp, D_MODEL)
    y_sample = xs.reshape(bs, ss, D_MODEL)
    new_k = new_kv[0].reshape(bp, DEPTH, sp, KV_A, HEAD_DIM)
    new_v = new_kv[1].reshape(bp, DEPTH, sp, KV_A, HEAD_DIM)
    return (y_prompt, y_sample, new_k, new_v, new_s)
```

```python
import functools

import jax
import jax.numpy as jnp
import numpy as np
from jax import lax
from jax.experimental import pallas as pl
from jax.experimental.pallas import tpu as pltpu

F32 = jnp.float32
BF16 = jnp.bfloat16

D_MODEL = 1024
DEPTH = 4
GRID_W = 64
D_FF = 4 * D_MODEL
H_R = 4
DK_R = 128
DV_R = 128
W_R = H_R * DV_R
H_A = 8
KV_A = 2
HEAD_DIM = 64
W_A = H_A * HEAD_DIM
ROPE_AXIS_DIM = HEAD_DIM // 2
ROPE_THETA = 10000.0
EPS = 1e-6
LOG2E = 1.4426950408889634
IN_R = 3 * H_R * DK_R + 2 * W_R
IN_A = W_A + 2 * KV_A * HEAD_DIM
MOD_W = 6 * D_MODEL
MOD_ROWS = 8

LANES = 128
SUBLANES = 8
VMEM_LIMIT = 56 * 1024 * 1024

HGRN_CHUNK = 128
FAST_CHUNK = 64
FAST_UNROLL = 8
FAST_LIMIT = 115.0
PROJ_ROWS = 256
HGRN_ROWS = 512
ATTN_QROWS = 256
ATTN_KEYS = 256
ATTN_ROWS = 1024
ATTN_LOOKAHEAD = 2
FFN_CHUNK = 1024
ROW_TILE = 512


def _dot(a, b):
    return jnp.dot(a, b, preferred_element_type=F32)


def _dot_nt(a, b):
    return lax.dot_general(a, b, (((1,), (1,)), ((), ())), preferred_element_type=F32)


def _params(*sem):
    return pltpu.CompilerParams(dimension_semantics=sem, vmem_limit_bytes=VMEM_LIMIT)


def _rms(x):
    return x * lax.rsqrt(jnp.mean(x * x, axis=-1, keepdims=True) + EPS)


def _mod_spec(layer, rows_per_seq, tile, ctx):
    per = rows_per_seq // tile
    row = (lambda i: MOD_ROWS // 2) if ctx else (lambda i: i // per)
    return pl.BlockSpec((None, None, 1, MOD_W), lambda i: (layer, row(i), 0, 0))


def _layer_spec(a, layer):
    return pl.BlockSpec((None,) + a.shape[1:], lambda i: (layer, 0, 0))


def _prenorm(x, shift, scale, g):
    return (_rms(x) * g * (1.0 + scale) + shift).astype(BF16)


def _mod_kernel(c_ref, w_ref, b_ref, o_ref):
    c = c_ref[...]
    s = c / (1.0 + jnp.exp(-c))
    o_ref[...] = _dot(s.astype(BF16), w_ref[...].astype(BF16)) + b_ref[...]


def _modulation(cvec, w_mod, b_mod):
    tn = 1536
    return pl.pallas_call(
        _mod_kernel,
        out_shape=jax.ShapeDtypeStruct((DEPTH, MOD_ROWS, MOD_W), F32),
        grid=(DEPTH, MOD_W // tn),
        in_specs=[
            pl.BlockSpec((MOD_ROWS, D_MODEL), lambda l, j: (0, 0)),
            pl.BlockSpec((None, D_MODEL, tn), lambda l, j: (l, 0, j)),
            pl.BlockSpec((None, 1, tn), lambda l, j: (l, 0, j)),
        ],
        out_specs=pl.BlockSpec((None, MOD_ROWS, tn), lambda l, j: (l, 0, j)),
        compiler_params=_params("arbitrary", "arbitrary"),
        name="modulation",
    )(cvec, w_mod, b_mod.reshape(DEPTH, 1, MOD_W))


def _prenorm_kernel(x_ref, mod_ref, g_ref, o_ref):
    o_ref[...] = _prenorm(x_ref[...], mod_ref[:, 0:D_MODEL], mod_ref[:, D_MODEL:2 * D_MODEL], g_ref[...])


def _first_prenorm(x, mod, g1, rows_per_seq, ctx):
    n = x.shape[0]
    tm = ROW_TILE
    return pl.pallas_call(
        _prenorm_kernel,
        out_shape=jax.ShapeDtypeStruct((n, D_MODEL), BF16),
        grid=(n // tm,),
        in_specs=[pl.BlockSpec((tm, D_MODEL), lambda i: (i, 0)),
                  _mod_spec(0, rows_per_seq, tm, ctx),
                  _layer_spec(g1, 0)],
        out_specs=pl.BlockSpec((tm, D_MODEL), lambda i: (i, 0)),
        compiler_params=_params("arbitrary"),
        name="prenorm",
    )(x, mod, g1)


def _bcast_block_row(b, blk, r):
    n, w = b.shape
    parts = [jnp.broadcast_to(b[k * blk + r:k * blk + r + 1, :], (blk, w)) for k in range(n // blk)]
    return parts[0] if len(parts) == 1 else jnp.concatenate(parts, axis=0)


def _hgrn_kernel(*refs, layer, seq, nseq, has_s0, has_prev, want_s):
    L = HGRN_CHUNK
    n_chunks = seq // L
    n_levels = int(np.log2(L))
    refs = list(refs)
    h_ref, w_ref, lbraw_ref, gn_ref = refs[:4]
    pos = 4
    s0_ref = None
    if has_s0:
        s0_ref = refs[pos]
        pos += 1
    if has_prev:
        pos += 1
    o_ref = refs[pos]
    pos += 1
    s_ref = None
    if want_s:
        s_ref = refs[pos]
        pos += 1
    proj_ref, of_ref, ob_ref, st_ref, code_ref, kg_ref, b_ref = refs[pos:pos + 7]
    q_cols, i_cols, g_cols = (slice(n * W_R, (n + 1) * W_R) for n in (0, 3, 4))
    f_cols = [slice(W_R, 2 * W_R), slice(2 * W_R, 3 * W_R)]

    def project(g):
        rows = slice(g * PROJ_ROWS, (g + 1) * PROJ_ROWS)
        proj_ref[rows, :] = _dot(h_ref[rows, :], w_ref[...])

    if layer > 0:
        raw = lbraw_ref[...]
        e = jnp.exp(raw - jnp.max(raw, axis=0, keepdims=True))
        p = e / jnp.sum(e, axis=0, keepdims=True)
        lb = p[1]
        for j in range(2, layer + 1):
            lb = lb + p[j]
        log_lb = jnp.log(lb)
        log_1m_lb = jnp.log1p(-lb)
        one_m_lb = 1.0 - lb

    def gates(fx, d):
        ls = jnp.minimum(fx, 0.0) - jnp.log(1.0 + jnp.exp(-jnp.abs(fx)))
        k = jnp.exp(ls - fx)
        if layer == 0:
            return ls, k
        a = log_lb[d:d + 1, :]
        c = log_1m_lb[d:d + 1, :] + ls
        lf = jnp.maximum(a, c) + jnp.log(1.0 + jnp.exp(-jnp.abs(a - c)))
        return lf, one_m_lb[d:d + 1, :] * k

    ti = lax.broadcasted_iota(jnp.int32, (L, L), 0)
    si = lax.broadcasted_iota(jnp.int32, (L, L), 1)
    x = ti ^ si
    lv = jnp.zeros((L, L), jnp.int32)
    for j in range(n_levels):
        lv = lv + jnp.where((x >> j) != 0, 1, 0)
    code_ref[0] = jnp.where(ti >= si, lv, -1)
    code_ref[1] = jnp.where(ti <= si, lv, -1)

    def init_states():
        for s in range(nseq):
            for d in range(2):
                for h in range(H_R):
                    if has_s0:
                        st_ref[s, d, h] = s0_ref[s, d, h].T
                    else:
                        st_ref[s, d, h] = jnp.zeros((DV_R, DK_R), F32)

    init_states()

    def cumulative(lf, tri):
        hi = lf.astype(BF16)
        r1 = lf - hi.astype(F32)
        mid = r1.astype(BF16)
        lo = (r1 - mid.astype(F32)).astype(BF16)
        b3 = _dot(tri, jnp.concatenate([hi, mid, lo], axis=1))
        return b3[:, 0:W_R] + b3[:, W_R:2 * W_R] + b3[:, 2 * W_R:3 * W_R]

    Lf = FAST_CHUNK
    n_fast = seq // Lf
    unroll = min(FAST_UNROLL, n_fast)
    ref_row = [Lf // 2 - 1, Lf // 2]
    first_row = [0, Lf - 1]
    last_row = [Lf - 1, 0]
    tf = lax.broadcasted_iota(jnp.int32, (Lf, Lf), 0)
    sf = lax.broadcasted_iota(jnp.int32, (Lf, Lf), 1)
    keep = [tf >= sf, tf <= sf]
    tf2 = lax.broadcasted_iota(jnp.int32, (Lf, 2 * Lf), 0)
    sf2 = lax.broadcasted_iota(jnp.int32, (Lf, 2 * Lf), 1) & (Lf - 1)
    keep2 = [tf2 >= sf2, tf2 <= sf2]

    def prepare(g, worst):
        for i in range(PROJ_ROWS // Lf):
            rows = slice(g * PROJ_ROWS + i * Lf, g * PROJ_ROWS + (i + 1) * Lf)
            q_top = jnp.max(jnp.abs(proj_ref[rows, q_cols]), axis=0, keepdims=True)
            q_bits = jnp.log(jnp.maximum(q_top, 1.0)) * LOG2E
            for d in range(2):
                lf, k = gates(proj_ref[rows, f_cols[d]], d)
                b = cumulative(lf * LOG2E, jnp.where(keep[d], 1.0, 0.0).astype(BF16))
                kg_ref[d, rows, :] = k
                b_ref[d, rows, :] = b
                b_mid = b[ref_row[d]:ref_row[d] + 1, :]
                grow = jnp.maximum(b[first_row[d]:first_row[d] + 1, :] - b_mid,
                                   b_mid - b[last_row[d]:last_row[d] + 1, :])
                worst = jnp.maximum(worst, grow + q_bits)
        return worst

    def fast_direction(d, s, c, out_ref):
        start = s * seq + c * Lf
        rows = slice(start, start + Lf) if isinstance(start, int) else pl.ds(pl.multiple_of(start, Lf), Lf)
        q = proj_ref[rows, q_cols]
        v = proj_ref[rows, i_cols]
        k = kg_ref[d, rows, :]
        b = b_ref[d, rows, :]
        b_mid = b[ref_row[d]:ref_row[d] + 1, :]
        edge = b[last_row[d]:last_row[d] + 1, :]
        q_t = (q * jnp.exp2(b - b_mid)).astype(BF16)
        k_f = k * jnp.exp2(b_mid - b)
        k_t = k_f.astype(BF16)
        k_out = (k_f * jnp.exp2(edge - b_mid)).astype(BF16)
        dec = jnp.exp2(edge)
        from_start = jnp.exp2(b_mid)
        vb = v.astype(BF16)
        zero_s = jnp.zeros((DV_R, DK_R), BF16)
        zero_r = jnp.zeros((Lf, DK_R), BF16)

        def side_by_side(x1, x2, zero):
            return jnp.concatenate([jnp.concatenate([x1, zero], axis=1),
                                    jnp.concatenate([zero, x2], axis=1)], axis=0)

        both = []
        for h in range(0, H_R, 2):
            c1 = slice(h * DK_R, (h + 1) * DK_R)
            c2 = slice((h + 1) * DK_R, (h + 2) * DK_R)
            cp = slice(h * DK_R, (h + 2) * DK_R)
            st1 = st_ref[s, d, h]
            st2 = st_ref[s, d, h + 1]
            rhs = jnp.concatenate([
                side_by_side((st1 * from_start[:, c1]).astype(BF16), (st2 * from_start[:, c2]).astype(BF16), zero_s),
                side_by_side(k_t[:, c1], k_t[:, c2], zero_r)], axis=0)
            both.append(_dot_nt(q_t[:, cp], rhs))
            v_rows = jnp.concatenate([v[:, c1], v[:, c2]], axis=0)
            upd = _dot(v_rows.T.astype(BF16), side_by_side(k_out[:, c1], k_out[:, c2], zero_r))
            st_ref[s, d, h] = st1 * dec[:, c1] + upd[:, 0:DK_R]
            st_ref[s, d, h + 1] = st2 * dec[:, c2] + upd[:, DK_R:2 * DK_R]
        yield
        outs = []
        for n, h in enumerate(range(0, H_R, 2)):
            c1 = slice(h * DK_R, (h + 1) * DK_R)
            c2 = slice((h + 1) * DK_R, (h + 2) * DK_R)
            a = jnp.where(keep2[d], both[n][:, 2 * DV_R:2 * DV_R + 2 * Lf], 0.0).astype(BF16)
            outs.append(both[n][:, 0:2 * DV_R] + _dot(a, side_by_side(vb[:, c1], vb[:, c2], zero_r)))
        out_ref[rows, :] = jnp.concatenate(outs, axis=1)

    def in_turn(gens):
        active = list(gens)
        while active:
            for gen in list(active):
                if next(gen, "done") == "done":
                    active.remove(gen)

    def fast_chunks(s, first):
        gens = []
        for i in range(unroll):
            gens.append(fast_direction(0, s, first + i, of_ref))
            gens.append(fast_direction(1, s, n_fast - 1 - (first + i), ob_ref))
        in_turn(gens)

    n_groups = nseq * seq // PROJ_ROWS
    inline = n_fast == unroll
    worst = jnp.zeros((1, W_R), F32)
    project(0)
    for g in range(n_groups):
        if g + 1 < n_groups:
            project(g + 1)
        worst = prepare(g, worst)
        if inline and ((g + 1) * PROJ_ROWS) % seq == 0:
            fast_chunks((g + 1) * PROJ_ROWS // seq - 1, 0)
    if not inline:
        for s in range(nseq):
            def fast_step(c, carry, s=s):
                fast_chunks(s, c * unroll)
                return carry
            lax.fori_loop(0, n_fast // unroll, fast_step, 0)

    def direction(d, s, c, out_ref):
        rows = pl.ds(pl.multiple_of(s * seq + c * L, L), L)
        row = lax.broadcasted_iota(jnp.int32, (L, W_R), 0)
        code = code_ref[d]
        tri = jnp.where(code >= 0, 1.0, 0.0).astype(BF16)
        q = proj_ref[rows, q_cols]
        v = proj_ref[rows, i_cols]
        lf, k = gates(proj_ref[rows, f_cols[d]], d)
        lf = lf * LOG2E
        b = cumulative(lf, tri)
        yield

        vb = v.astype(BF16)
        a_h = [None] * H_R
        for j in range(1, n_levels + 1):
            half = 1 << (j - 1)
            blk = 2 * half
            u = row & (blk - 1)
            qside = (u >= half) if d == 0 else (u < half)
            if j == 1:
                e_j = jnp.where(qside, lf, 0.0)
            elif j == 2:
                up = pltpu.roll(lf, L - 1, axis=0)
                dn = pltpu.roll(lf, 1, axis=0)
                if d == 0:
                    e_j = jnp.where(u == 0, up, jnp.where(u == 1, 0.0, jnp.where(u == 2, lf, lf + dn)))
                else:
                    e_j = jnp.where(u == 0, lf + up, jnp.where(u == 1, lf, jnp.where(u == 2, 0.0, dn)))
            else:
                diff = b - _bcast_block_row(b, blk, half - 1 + d)
                e_j = jnp.where(qside, diff, -diff)
            z = (jnp.where(qside, q, k) * jnp.exp2(e_j)).astype(BF16)
            for h in range(H_R):
                cs = slice(h * DK_R, (h + 1) * DK_R)
                p_j = _dot_nt(z[:, cs], z[:, cs])
                a_h[h] = jnp.where(code == j, p_j, 0.0 if a_h[h] is None else a_h[h])
            yield

        edge = b[L - 1:L, :] if d == 0 else b[0:1, :]
        q_in = (q * jnp.exp2(b)).astype(BF16)
        k_out = (k * jnp.exp2(edge - b)).astype(BF16)
        dec = jnp.exp2(edge)
        qk = q * k
        outs = []
        for h in range(H_R):
            cs = slice(h * DK_R, (h + 1) * DK_R)
            st = st_ref[s, d, h]
            o = _dot(a_h[h].astype(BF16), vb[:, cs]) + _dot_nt(q_in[:, cs], st.astype(BF16))
            o = o + jnp.sum(qk[:, cs], axis=-1, keepdims=True) * v[:, cs]
            st_ref[s, d, h] = st * dec[:, cs] + _dot(v[:, cs].T.astype(BF16), k_out[:, cs])
            outs.append(o)
        out_ref[rows, :] = jnp.concatenate(outs, axis=1)

    def step(i, carry):
        s, c = i // n_chunks, i % n_chunks
        in_turn([direction(0, s, c, of_ref), direction(1, s, n_chunks - 1 - c, ob_ref)])
        return carry

    @pl.when(jnp.max(worst) > FAST_LIMIT)
    def _():
        init_states()
        lax.fori_loop(0, nseq * n_chunks, step, 0)

    def finish(c, carry):
        rows = pl.ds(pl.multiple_of(c * L, L), L)
        o = of_ref[rows, :] + ob_ref[rows, :]
        g = proj_ref[rows, g_cols]
        parts = [_rms(o[:, h * DV_R:(h + 1) * DV_R]) for h in range(H_R)]
        y = jnp.concatenate(parts, axis=1) * gn_ref[...]
        o_ref[rows, :] = y * (g / (1.0 + jnp.exp(-g)))
        return carry

    lax.fori_loop(0, nseq * n_chunks, finish, 0)

    if want_s:
        for s in range(nseq):
            for d in range(2):
                for h in range(H_R):
                    s_ref[s, d, h] = st_ref[s, d, h].T


def _hgrn(hb, w_in, lb_raw, gn, layer, batch, seq, s0=None, s_prev=None, want_s=False):
    has_s0 = s0 is not None
    nseq = max(1, HGRN_ROWS // seq)
    rows = nseq * seq
    state_spec = pl.BlockSpec((nseq, None, 2, H_R, DK_R, DV_R), lambda b: (b, layer, 0, 0, 0, 0))
    in_specs = [pl.BlockSpec((rows, D_MODEL), lambda b: (b, 0)),
                pl.BlockSpec((None, D_MODEL, IN_R), lambda b: (layer, 0, 0)),
                pl.BlockSpec((DEPTH, 2, W_R), lambda b: (0, 0, 0)),
                _layer_spec(gn, layer)]
    args = [hb, w_in, lb_raw, gn]
    if has_s0:
        in_specs.append(state_spec)
        args.append(s0)
    out_shape = [jax.ShapeDtypeStruct((batch * seq, W_R), F32)]
    out_specs = [pl.BlockSpec((rows, W_R), lambda b: (b, 0))]
    aliases = {}
    if want_s:
        out_shape.append(jax.ShapeDtypeStruct((batch, DEPTH, 2, H_R, DK_R, DV_R), F32))
        out_specs.append(state_spec)
        if s_prev is not None:
            aliases[len(args)] = 1
            in_specs.append(pl.BlockSpec(memory_space=pl.ANY))
            args.append(s_prev)
    res = pl.pallas_call(
        functools.partial(_hgrn_kernel, layer=layer, seq=seq, nseq=nseq, has_s0=has_s0,
                          has_prev=bool(aliases), want_s=want_s),
        out_shape=out_shape,
        grid=(batch // nseq,),
        in_specs=in_specs,
        out_specs=out_specs,
        scratch_shapes=[pltpu.VMEM((rows, IN_R), F32),
                        pltpu.VMEM((rows, W_R), F32), pltpu.VMEM((rows, W_R), F32),
                        pltpu.VMEM((nseq, 2, H_R, DV_R, DK_R), F32),
                        pltpu.VMEM((2, HGRN_CHUNK, HGRN_CHUNK), jnp.int32),
                        pltpu.VMEM((2, rows, W_R), F32), pltpu.VMEM((2, rows, W_R), F32)],
        input_output_aliases=aliases,
        compiler_params=_params("arbitrary"),
        name="hgrn",
    )(*args)
    return res if want_s else (res[0], None)


def _head_mean_square(x):
    w = x.shape[1]
    shift = HEAD_DIM.bit_length() - 1
    r = lax.broadcasted_iota(jnp.int32, (w, w), 0) >> shift
    c = lax.broadcasted_iota(jnp.int32, (w, w), 1) >> shift
    ones = jnp.where(r == c, 1.0, 0.0).astype(BF16)
    return _dot((x * x).astype(BF16), ones) * (1.0 / HEAD_DIM)


def _attn_kernel(*refs, seq, nseq, past, n_prev):
    refs = list(refs)
    h_ref, wq0_ref, wq1_ref, wkv_ref, qg_ref, kg_ref = refs[:6]
    pos = 6
    if past:
        ck_ref, cv_ref, cos_ref, sa_ref, sb_ref = refs[pos:pos + 5]
        pos += 5
        o_ref = refs[pos]
        pos += 1
    else:
        pos += n_prev
        o_ref, kn_ref, vn_ref = refs[pos:pos + 3]
        pos += 3
    qs_ref, ks_ref, vt_ref = refs[pos:pos + 3]

    def rope(x):
        up = pltpu.roll(x, LANES - ROPE_AXIS_DIM // 2, axis=1)
        dn = pltpu.roll(x, ROPE_AXIS_DIM // 2, axis=1)
        return x * cos_ref[...] + up * sa_ref[...] + dn * sb_ref[...]

    hb = h_ref[...]
    q = jnp.concatenate([_dot(hb, wq0_ref[...]), _dot(hb, wq1_ref[...])], axis=1)
    kv = _dot(hb, wkv_ref[...])
    k = kv[:, 0:LANES]
    v = kv[:, LANES:2 * LANES]
    qn = q * lax.rsqrt(_head_mean_square(q) + EPS) * qg_ref[...]
    kn = k * lax.rsqrt(_head_mean_square(k) + EPS) * kg_ref[...]
    scale = HEAD_DIM ** -0.5 * LOG2E
    low = lax.broadcasted_iota(jnp.int32, (nseq * seq, LANES), 1) < HEAD_DIM
    pairs_per_kv = (W_A // LANES) // KV_A
    for p in range(W_A // LANES):
        cs = slice(p * LANES, (p + 1) * LANES)
        qp = (rope(qn[:, cs]) if past else qn[:, cs]) * scale
        even = jnp.where(low, qp, 0.0)
        odd = jnp.where(low, 0.0, qp)
        if p // pairs_per_kv == 0:
            odd = pltpu.roll(odd, HEAD_DIM, axis=1)
        else:
            even = pltpu.roll(even, HEAD_DIM, axis=1)
        qs_ref[0, :, cs] = even.astype(BF16)
        qs_ref[1, :, cs] = odd.astype(BF16)
    if past:
        ks_ref[0] = jnp.concatenate([ck_ref[...], rope(kn)], axis=0).astype(BF16)
        vt_ref[0] = jnp.concatenate([cv_ref[...], v], axis=0).T.astype(BF16)
    else:
        for i in range(nseq):
            rs = slice(i * seq, (i + 1) * seq)
            ks_ref[i] = kn[rs, :].astype(BF16)
            vt_ref[i] = v[rs, :].T.astype(BF16)
            kn_ref[i] = kn[rs, :]
            vn_ref[i] = v[rs, :]

    tq = min(ATTN_QROWS, seq)
    total = past + seq

    kb = ATTN_KEYS if total % ATTN_KEYS == 0 else total
    n_blocks = total // kb
    n_tiles = seq // tq
    seqs = range(nseq) if n_tiles == 1 else range(1)
    steps = [(i, p, kc) for i in seqs for p in range(W_A // LANES) for kc in range(n_blocks)]

    def tile(r, carry):

        def q_rows(i):
            start = i * seq + r * tq
            return slice(start, start + tq) if isinstance(start, int) else pl.ds(pl.multiple_of(start, tq), tq)

        def scores(step):
            i, p, kc = step
            cs = slice(p * LANES, (p + 1) * LANES)
            q2 = jnp.concatenate([qs_ref[0, q_rows(i), cs], qs_ref[1, q_rows(i), cs]], axis=0)
            return _dot_nt(ks_ref[i, kc * kb:(kc + 1) * kb, :], q2)

        ahead = [scores(st) for st in steps[:ATTN_LOOKAHEAD]]
        for n, (i, p, kc) in enumerate(steps):
            s = ahead.pop(0)
            if n + ATTN_LOOKAHEAD < len(steps):
                ahead.append(scores(steps[n + ATTN_LOOKAHEAD]))
            kh = p // pairs_per_kv
            v_t = vt_ref[i, kh * HEAD_DIM:(kh + 1) * HEAD_DIM, kc * kb:(kc + 1) * kb]
            s_max = jnp.max(s, axis=0, keepdims=True)
            if kc == 0:
                m = s_max
                pexp = jnp.exp2(s - m)
                den = jnp.sum(pexp, axis=0, keepdims=True)
                o2 = _dot(v_t, pexp.astype(BF16))
            else:
                m_new = jnp.maximum(m, s_max)
                alpha = jnp.exp2(m - m_new)
                pexp = jnp.exp2(s - m_new)
                den = den * alpha + jnp.sum(pexp, axis=0, keepdims=True)
                o2 = o2 * alpha + _dot(v_t, pexp.astype(BF16))
                m = m_new
            if kc == n_blocks - 1:
                o2 = o2 / den
                o_ref[q_rows(i), p * LANES:(p + 1) * LANES] = jnp.concatenate([o2[:, 0:tq], o2[:, tq:2 * tq]], axis=0).T
        return carry

    if n_tiles == 1:
        tile(0, 0)
    else:
        assert nseq == 1
        lax.fori_loop(0, n_tiles, tile, 0)


def _attn(hb, w_in, qg, kg, batch, seq, layer, cache=None, rope_tabs=None, kv_prev=None):
    past = 0 if cache is None else cache[0].shape[2]
    wide = 2 * LANES
    w_specs = [pl.BlockSpec((None, D_MODEL, wide), lambda b, j=j: (layer, 0, IN_R // wide + j))
               for j in range(IN_A // wide)]
    nseq = max(1, ATTN_ROWS // seq)
    rows = nseq * seq
    in_specs = [pl.BlockSpec((rows, D_MODEL), lambda b: (b, 0)), *w_specs,
                _layer_spec(qg, layer), _layer_spec(kg, layer)]
    args = [hb, w_in, w_in, w_in, qg, kg]
    out_shape = [jax.ShapeDtypeStruct((batch * seq, W_A), F32)]
    out_specs = [pl.BlockSpec((rows, W_A), lambda b: (b, 0))]
    aliases = {}
    if past:
        ck, cv = cache
        cspec = pl.BlockSpec((None, None, past, LANES), lambda b: (b, layer, 0, 0))
        tspec = pl.BlockSpec((seq, LANES), lambda b: (0, 0))
        in_specs += [cspec, cspec, tspec, tspec, tspec]
        args += [ck, cv, *rope_tabs]
    else:
        for i in range(2):
            out_shape.append(jax.ShapeDtypeStruct((batch, DEPTH, seq, LANES), F32))
            out_specs.append(pl.BlockSpec((nseq, None, seq, LANES), lambda b: (b, layer, 0, 0)))
            if kv_prev is not None:
                aliases[len(args)] = 1 + i
                in_specs.append(pl.BlockSpec(memory_space=pl.ANY))
                args.append(kv_prev[i])
    res = pl.pallas_call(
        functools.partial(_attn_kernel, seq=seq, nseq=nseq, past=past, n_prev=len(aliases)),
        out_shape=out_shape,
        grid=(batch // nseq,),
        in_specs=in_specs,
        out_specs=out_specs,
        scratch_shapes=[pltpu.VMEM((2, rows, W_A), BF16),
                        pltpu.VMEM((nseq, past + seq, LANES), BF16),
                        pltpu.VMEM((nseq, LANES, past + seq), BF16)],
        input_output_aliases=aliases,
        compiler_params=_params("arbitrary"),
        name="attn",
    )(*args)
    return res


def _outffn_kernel(*refs, final):
    refs = list(refs)
    x_ref, or_ref, oa_ref, mod_ref, g2_ref, wout_ref, w1_ref, w2_ref, last_ref = refs[:9]
    pos = 9
    if not final:
        nmod_ref = refs[pos]
        pos += 1
    o_ref = refs[pos]
    gate1 = mod_ref[:, 2 * D_MODEL:3 * D_MODEL]
    shift2 = mod_ref[:, 3 * D_MODEL:4 * D_MODEL]
    scale2 = mod_ref[:, 4 * D_MODEL:5 * D_MODEL]
    gate2 = mod_ref[:, 5 * D_MODEL:6 * D_MODEL]
    mix = _dot(or_ref[...].astype(BF16), wout_ref[0:W_R, :]) + _dot(oa_ref[...].astype(BF16), wout_ref[W_R:W_R + W_A, :])
    x1 = x_ref[...] + gate1 * mix
    h2 = _prenorm(x1, shift2, scale2, g2_ref[...])
    acc = jnp.zeros_like(x1)
    for j in range(D_FF // FFN_CHUNK):
        cs = slice(j * FFN_CHUNK, (j + 1) * FFN_CHUNK)
        hid = jnp.maximum(_dot(h2, w1_ref[:, cs]), 0.0)
        acc = acc + _dot((hid * hid).astype(BF16), w2_ref[cs, :])
    x2 = x1 + gate2 * acc
    if final:
        o_ref[...] = _rms(x2) * last_ref[...]
    else:
        o_ref[...] = x2
        hn_ref = refs[pos + 1]
        hn_ref[...] = _prenorm(x2, nmod_ref[:, 0:D_MODEL], nmod_ref[:, D_MODEL:2 * D_MODEL], last_ref[...])


def _outffn(x, o_r, o_a, mod, g1, g2, wout, w1, w2, gf, layer, rows_per_seq, ctx):
    n = x.shape[0]
    tm = ROW_TILE
    final = layer == DEPTH - 1
    row = lambda w: pl.BlockSpec((tm, w), lambda i: (i, 0))
    in_specs = [row(D_MODEL), row(W_R), row(W_A),
                _mod_spec(layer, rows_per_seq, tm, ctx),
                _layer_spec(g2, layer), _layer_spec(wout, layer), _layer_spec(w1, layer), _layer_spec(w2, layer)]
    args = [x, o_r, o_a, mod, g2, wout, w1, w2]
    out_shape = [jax.ShapeDtypeStruct((n, D_MODEL), F32)]
    out_specs = [row(D_MODEL)]
    if final:
        in_specs.append(pl.BlockSpec(gf.shape, lambda i: (0, 0)))
        args.append(gf)
    else:
        in_specs += [_layer_spec(g1, layer + 1), _mod_spec(layer + 1, rows_per_seq, tm, ctx)]
        args += [g1, mod]
        out_shape.append(jax.ShapeDtypeStruct((n, D_MODEL), BF16))
        out_specs.append(row(D_MODEL))
    res = pl.pallas_call(
        functools.partial(_outffn_kernel, final=final),
        out_shape=out_shape,
        grid=(n // tm,),
        in_specs=in_specs,
        out_specs=out_specs,
        compiler_params=_params("arbitrary"),
        name="outffn",
    )(*args)
    return (res[0], None) if final else res


def _rope_tables(seq):
    rows = seq // GRID_W
    rowi = jnp.repeat(jnp.arange(rows, dtype=F32), GRID_W)
    coli = jnp.tile(jnp.arange(GRID_W, dtype=F32), rows)
    inv = ROPE_THETA ** (-jnp.arange(0, ROPE_AXIS_DIM, 2, dtype=F32) / ROPE_AXIS_DIM)
    ar = rowi[:, None] * inv[None, :]
    ac = coli[:, None] * inv[None, :]
    ang = jnp.concatenate([ar, ar, ac, ac], axis=-1)
    cos = jnp.tile(jnp.cos(ang), (1, LANES // HEAD_DIM))
    sin = jnp.tile(jnp.sin(ang), (1, LANES // HEAD_DIM))
    first = (jnp.arange(LANES) % ROPE_AXIS_DIM) < ROPE_AXIS_DIM // 2
    return cos, jnp.where(first, -sin, 0.0), jnp.where(first, 0.0, sin)


def kernel(x_prompt, x_sample, cache_k, cache_v, state_hgrn, c, c_ctx, w_mod, b_mod, norm1_g, w_in, lb_raw,
           hgrn_norm_g, q_norm_g, k_norm_g, w_out, norm2_g, w1, w2, final_norm_g):
    bp, sp, _ = x_prompt.shape
    bs, ss, _ = x_sample.shape
    past = cache_k.shape[2]

    cvec = jnp.zeros((MOD_ROWS, D_MODEL), F32).at[0:bs].set(c).at[MOD_ROWS // 2].set(c_ctx)
    mod = _modulation(cvec, w_mod, b_mod).reshape(DEPTH, MOD_ROWS, 1, MOD_W)

    w_in_b, w_out_b, w1_b, w2_b = (w.astype(BF16) for w in (w_in, w_out, w1, w2))
    ck = cache_k.reshape(bs, DEPTH, past, KV_A * HEAD_DIM)
    cv = cache_v.reshape(bs, DEPTH, past, KV_A * HEAD_DIM)
    tabs = _rope_tables(ss)
    gf = final_norm_g.reshape(1, D_MODEL)
    g1 = norm1_g.reshape(DEPTH, 1, D_MODEL)
    g2 = norm2_g.reshape(DEPTH, 1, D_MODEL)
    gn = jnp.tile(hgrn_norm_g, (1, H_R)).reshape(DEPTH, 1, W_R)
    qg = jnp.tile(q_norm_g, (1, H_A)).reshape(DEPTH, 1, W_A)
    kg = jnp.tile(k_norm_g, (1, KV_A)).reshape(DEPTH, 1, LANES)

    xp = x_prompt.reshape(bp * sp, D_MODEL)
    xs = x_sample.reshape(bs * ss, D_MODEL)
    hp = _first_prenorm(xp, mod, g1, sp, True)
    hs = _first_prenorm(xs, mod, g1, ss, False)
    new_s = jnp.zeros((bp, DEPTH, 2, H_R, DK_R, DV_R), F32)
    new_kv = [jnp.zeros((bp, DEPTH, sp, KV_A * HEAD_DIM), F32) for _ in range(2)]
    for l in range(DEPTH):
        o_r, new_s = _hgrn(hp, w_in_b, lb_raw, gn, l, bp, sp, s_prev=new_s, want_s=True)
        o_a, *new_kv = _attn(hp, w_in_b, qg, kg, bp, sp, l, kv_prev=new_kv)
        xp, hp = _outffn(xp, o_r, o_a, mod, g1, g2, w_out_b, w1_b, w2_b, gf, l, sp, True)

        o_r, _ = _hgrn(hs, w_in_b, lb_raw, gn, l, bs, ss, s0=state_hgrn)
        o_a, = _attn(hs, w_in_b, qg, kg, bs, ss, l, cache=(ck, cv), rope_tabs=tabs)
        xs, hs = _outffn(xs, o_r, o_a, mod, g1, g2, w_out_b, w1_b, w2_b, gf, l, ss, False)

    y_prompt = xp.reshape(bp, sp, D_MODEL)
    y_sample = xs.reshape(bs, ss, D_MODEL)
    new_k = new_kv[0].reshape(bp, DEPTH, sp, KV_A, HEAD_DIM)
    new_v = new_kv[1].reshape(bp, DEPTH, sp, KV_A, HEAD_DIM)
    return (y_prompt, y_sample, new_k, new_v, new_s)
```

```python
import functools

import jax
import jax.numpy as jnp
import numpy as np
from jax import lax
from jax.experimental import pallas as pl
from jax.experimental.pallas import tpu as pltpu

F32 = jnp.float32
BF16 = jnp.bfloat16

D_MODEL = 1024
DEPTH = 4
GRID_W = 64
D_FF = 4 * D_MODEL
H_R = 4
DK_R = 128
DV_R = 128
W_R = H_R * DV_R
H_A = 8
KV_A = 2
HEAD_DIM = 64
W_A = H_A * HEAD_DIM
ROPE_AXIS_DIM = HEAD_DIM // 2
ROPE_THETA = 10000.0
EPS = 1e-6
LOG2E = 1.4426950408889634
IN_R = 3 * H_R * DK_R + 2 * W_R
IN_A = W_A + 2 * KV_A * HEAD_DIM
MOD_W = 6 * D_MODEL
MOD_ROWS = 8

LANES = 128
SUBLANES = 8
VMEM_LIMIT = 56 * 1024 * 1024

HGRN_CHUNK = 128
FAST_CHUNK = 64
FAST_UNROLL = 8
FAST_LIMIT = 115.0
PROJ_ROWS = 256
HGRN_ROWS = 512
ATTN_QROWS = 256
ATTN_KEYS = 256
ATTN_ROWS = 1024
ATTN_LOOKAHEAD = 2
FFN_CHUNK = 1024
FFN_PARTS = 2
ROW_TILE = 512


def _dot(a, b):
    return jnp.dot(a, b, preferred_element_type=F32)


def _dot_nt(a, b):
    return lax.dot_general(a, b, (((1,), (1,)), ((), ())), preferred_element_type=F32)


def _in_turn(gens):
    active = list(gens)
    while active:
        for gen in list(active):
            if next(gen, "done") == "done":
                active.remove(gen)


def _params(*sem):
    return pltpu.CompilerParams(dimension_semantics=sem, vmem_limit_bytes=VMEM_LIMIT)


def _rms(x):
    return x * lax.rsqrt(jnp.mean(x * x, axis=-1, keepdims=True) + EPS)


def _mod_spec(layer, rows_per_seq, tile, ctx):
    per = rows_per_seq // tile
    row = (lambda i: MOD_ROWS // 2) if ctx else (lambda i: i // per)
    return pl.BlockSpec((None, None, 1, MOD_W), lambda i: (layer, row(i), 0, 0))


def _layer_spec(a, layer):
    return pl.BlockSpec((None,) + a.shape[1:], lambda i: (layer, 0, 0))


def _prenorm(x, shift, scale, g):
    return (_rms(x) * g * (1.0 + scale) + shift).astype(BF16)


def _mod_kernel(c_ref, w_ref, b_ref, o_ref):
    c = c_ref[...]
    s = c / (1.0 + jnp.exp(-c))
    o_ref[...] = _dot(s.astype(BF16), w_ref[...].astype(BF16)) + b_ref[...]


def _modulation(cvec, w_mod, b_mod):
    tn = 1536
    return pl.pallas_call(
        _mod_kernel,
        out_shape=jax.ShapeDtypeStruct((DEPTH, MOD_ROWS, MOD_W), F32),
        grid=(DEPTH, MOD_W // tn),
        in_specs=[
            pl.BlockSpec((MOD_ROWS, D_MODEL), lambda l, j: (0, 0)),
            pl.BlockSpec((None, D_MODEL, tn), lambda l, j: (l, 0, j)),
            pl.BlockSpec((None, 1, tn), lambda l, j: (l, 0, j)),
        ],
        out_specs=pl.BlockSpec((None, MOD_ROWS, tn), lambda l, j: (l, 0, j)),
        compiler_params=_params("arbitrary", "arbitrary"),
        name="modulation",
    )(cvec, w_mod, b_mod.reshape(DEPTH, 1, MOD_W))


def _prenorm_kernel(x_ref, mod_ref, g_ref, o_ref):
    o_ref[...] = _prenorm(x_ref[...], mod_ref[:, 0:D_MODEL], mod_ref[:, D_MODEL:2 * D_MODEL], g_ref[...])


def _first_prenorm(x, mod, g1, rows_per_seq, ctx):
    n = x.shape[0]
    tm = ROW_TILE
    return pl.pallas_call(
        _prenorm_kernel,
        out_shape=jax.ShapeDtypeStruct((n, D_MODEL), BF16),
        grid=(n // tm,),
        in_specs=[pl.BlockSpec((tm, D_MODEL), lambda i: (i, 0)),
                  _mod_spec(0, rows_per_seq, tm, ctx),
                  _layer_spec(g1, 0)],
        out_specs=pl.BlockSpec((tm, D_MODEL), lambda i: (i, 0)),
        compiler_params=_params("arbitrary"),
        name="prenorm",
    )(x, mod, g1)


def _bcast_block_row(b, blk, r):
    n, w = b.shape
    parts = [jnp.broadcast_to(b[k * blk + r:k * blk + r + 1, :], (blk, w)) for k in range(n // blk)]
    return parts[0] if len(parts) == 1 else jnp.concatenate(parts, axis=0)


def _hgrn_kernel(*refs, layer, seq, nseq, has_s0, has_prev, want_s):
    L = HGRN_CHUNK
    n_chunks = seq // L
    n_levels = int(np.log2(L))
    refs = list(refs)
    h_ref, w_ref, lbraw_ref, gn_ref = refs[:4]
    pos = 4
    s0_ref = None
    if has_s0:
        s0_ref = refs[pos]
        pos += 1
    if has_prev:
        pos += 1
    o_ref = refs[pos]
    pos += 1
    s_ref = None
    if want_s:
        s_ref = refs[pos]
        pos += 1
    proj_ref, of_ref, ob_ref, st_ref, code_ref, kg_ref, b_ref = refs[pos:pos + 7]
    q_cols, i_cols, g_cols = (slice(n * W_R, (n + 1) * W_R) for n in (0, 3, 4))
    f_cols = [slice(W_R, 2 * W_R), slice(2 * W_R, 3 * W_R)]

    def project(g):
        rows = slice(g * PROJ_ROWS, (g + 1) * PROJ_ROWS)
        proj_ref[rows, :] = _dot(h_ref[rows, :], w_ref[...])

    if layer > 0:
        raw = lbraw_ref[...]
        e = jnp.exp(raw - jnp.max(raw, axis=0, keepdims=True))
        p = e / jnp.sum(e, axis=0, keepdims=True)
        lb = p[1]
        for j in range(2, layer + 1):
            lb = lb + p[j]
        log_lb = jnp.log(lb)
        log_1m_lb = jnp.log1p(-lb)
        one_m_lb = 1.0 - lb

    def gates(fx, d):
        ls = jnp.minimum(fx, 0.0) - jnp.log(1.0 + jnp.exp(-jnp.abs(fx)))
        k = jnp.exp(ls - fx)
        if layer == 0:
            return ls, k
        a = log_lb[d:d + 1, :]
        c = log_1m_lb[d:d + 1, :] + ls
        lf = jnp.maximum(a, c) + jnp.log(1.0 + jnp.exp(-jnp.abs(a - c)))
        return lf, one_m_lb[d:d + 1, :] * k

    ti = lax.broadcasted_iota(jnp.int32, (L, L), 0)
    si = lax.broadcasted_iota(jnp.int32, (L, L), 1)
    x = ti ^ si
    lv = jnp.zeros((L, L), jnp.int32)
    for j in range(n_levels):
        lv = lv + jnp.where((x >> j) != 0, 1, 0)
    code_ref[0] = jnp.where(ti >= si, lv, -1)
    code_ref[1] = jnp.where(ti <= si, lv, -1)

    def init_states():
        for s in range(nseq):
            for d in range(2):
                for h in range(H_R):
                    if has_s0:
                        st_ref[s, d, h] = s0_ref[s, d, h].T
                    else:
                        st_ref[s, d, h] = jnp.zeros((DV_R, DK_R), F32)

    init_states()

    def cumulative(lf, tri):
        hi = lf.astype(BF16)
        r1 = lf - hi.astype(F32)
        mid = r1.astype(BF16)
        lo = (r1 - mid.astype(F32)).astype(BF16)
        b3 = _dot(tri, jnp.concatenate([hi, mid, lo], axis=1))
        return b3[:, 0:W_R] + b3[:, W_R:2 * W_R] + b3[:, 2 * W_R:3 * W_R]

    Lf = FAST_CHUNK
    n_fast = seq // Lf
    unroll = min(FAST_UNROLL, n_fast)
    ref_row = [Lf // 2 - 1, Lf // 2]
    first_row = [0, Lf - 1]
    last_row = [Lf - 1, 0]
    tf = lax.broadcasted_iota(jnp.int32, (Lf, Lf), 0)
    sf = lax.broadcasted_iota(jnp.int32, (Lf, Lf), 1)
    keep = [tf >= sf, tf <= sf]
    tf2 = lax.broadcasted_iota(jnp.int32, (Lf, 2 * Lf), 0)
    sf2 = lax.broadcasted_iota(jnp.int32, (Lf, 2 * Lf), 1) & (Lf - 1)
    keep2 = [tf2 >= sf2, tf2 <= sf2]

    def prepare(g, worst):
        for i in range(PROJ_ROWS // Lf):
            rows = slice(g * PROJ_ROWS + i * Lf, g * PROJ_ROWS + (i + 1) * Lf)
            q_top = jnp.max(jnp.abs(proj_ref[rows, q_cols]), axis=0, keepdims=True)
            q_bits = jnp.log(jnp.maximum(q_top, 1.0)) * LOG2E
            for d in range(2):
                lf, k = gates(proj_ref[rows, f_cols[d]], d)
                b = cumulative(lf * LOG2E, jnp.where(keep[d], 1.0, 0.0).astype(BF16))
                kg_ref[d, rows, :] = k
                b_ref[d, rows, :] = b
                b_mid = b[ref_row[d]:ref_row[d] + 1, :]
                grow = jnp.maximum(b[first_row[d]:first_row[d] + 1, :] - b_mid,
                                   b_mid - b[last_row[d]:last_row[d] + 1, :])
                worst = jnp.maximum(worst, grow + q_bits)
        return worst

    def fast_direction(d, s, c, out_ref):
        start = s * seq + c * Lf
        rows = slice(start, start + Lf) if isinstance(start, int) else pl.ds(pl.multiple_of(start, Lf), Lf)
        q = proj_ref[rows, q_cols]
        v = proj_ref[rows, i_cols]
        k = kg_ref[d, rows, :]
        b = b_ref[d, rows, :]
        b_mid = b[ref_row[d]:ref_row[d] + 1, :]
        edge = b[last_row[d]:last_row[d] + 1, :]
        q_t = (q * jnp.exp2(b - b_mid)).astype(BF16)
        k_f = k * jnp.exp2(b_mid - b)
        k_t = k_f.astype(BF16)
        k_out = (k_f * jnp.exp2(edge - b_mid)).astype(BF16)
        dec = jnp.exp2(edge)
        from_start = jnp.exp2(b_mid)
        vb = v.astype(BF16)
        zero_s = jnp.zeros((DV_R, DK_R), BF16)
        zero_r = jnp.zeros((Lf, DK_R), BF16)

        def side_by_side(x1, x2, zero):
            return jnp.concatenate([jnp.concatenate([x1, zero], axis=1),
                                    jnp.concatenate([zero, x2], axis=1)], axis=0)

        both = []
        for h in range(0, H_R, 2):
            c1 = slice(h * DK_R, (h + 1) * DK_R)
            c2 = slice((h + 1) * DK_R, (h + 2) * DK_R)
            cp = slice(h * DK_R, (h + 2) * DK_R)
            st1 = st_ref[s, d, h]
            st2 = st_ref[s, d, h + 1]
            rhs = jnp.concatenate([
                side_by_side((st1 * from_start[:, c1]).astype(BF16), (st2 * from_start[:, c2]).astype(BF16), zero_s),
                side_by_side(k_t[:, c1], k_t[:, c2], zero_r)], axis=0)
            both.append(_dot_nt(q_t[:, cp], rhs))
            v_rows = jnp.concatenate([v[:, c1], v[:, c2]], axis=0)
            upd = _dot(v_rows.T.astype(BF16), side_by_side(k_out[:, c1], k_out[:, c2], zero_r))
            st_ref[s, d, h] = st1 * dec[:, c1] + upd[:, 0:DK_R]
            st_ref[s, d, h + 1] = st2 * dec[:, c2] + upd[:, DK_R:2 * DK_R]
        yield
        outs = []
        for n, h in enumerate(range(0, H_R, 2)):
            c1 = slice(h * DK_R, (h + 1) * DK_R)
            c2 = slice((h + 1) * DK_R, (h + 2) * DK_R)
            a = jnp.where(keep2[d], both[n][:, 2 * DV_R:2 * DV_R + 2 * Lf], 0.0).astype(BF16)
            outs.append(both[n][:, 0:2 * DV_R] + _dot(a, side_by_side(vb[:, c1], vb[:, c2], zero_r)))
        out_ref[rows, :] = jnp.concatenate(outs, axis=1)

    def fast_chunks(s, first):
        gens = []
        for i in range(unroll):
            gens.append(fast_direction(0, s, first + i, of_ref))
            gens.append(fast_direction(1, s, n_fast - 1 - (first + i), ob_ref))
        _in_turn(gens)

    n_groups = nseq * seq // PROJ_ROWS
    inline = n_fast == unroll
    worst = jnp.zeros((1, W_R), F32)
    project(0)
    for g in range(n_groups):
        if g + 1 < n_groups:
            project(g + 1)
        worst = prepare(g, worst)
        if inline and ((g + 1) * PROJ_ROWS) % seq == 0:
            fast_chunks((g + 1) * PROJ_ROWS // seq - 1, 0)
    if not inline:
        for s in range(nseq):
            def fast_step(c, carry, s=s):
                fast_chunks(s, c * unroll)
                return carry
            lax.fori_loop(0, n_fast // unroll, fast_step, 0)

    def direction(d, s, c, out_ref):
        rows = pl.ds(pl.multiple_of(s * seq + c * L, L), L)
        row = lax.broadcasted_iota(jnp.int32, (L, W_R), 0)
        code = code_ref[d]
        tri = jnp.where(code >= 0, 1.0, 0.0).astype(BF16)
        q = proj_ref[rows, q_cols]
        v = proj_ref[rows, i_cols]
        lf, k = gates(proj_ref[rows, f_cols[d]], d)
        lf = lf * LOG2E
        b = cumulative(lf, tri)
        yield

        vb = v.astype(BF16)
        a_h = [None] * H_R
        for j in range(1, n_levels + 1):
            half = 1 << (j - 1)
            blk = 2 * half
            u = row & (blk - 1)
            qside = (u >= half) if d == 0 else (u < half)
            if j == 1:
                e_j = jnp.where(qside, lf, 0.0)
            elif j == 2:
                up = pltpu.roll(lf, L - 1, axis=0)
                dn = pltpu.roll(lf, 1, axis=0)
                if d == 0:
                    e_j = jnp.where(u == 0, up, jnp.where(u == 1, 0.0, jnp.where(u == 2, lf, lf + dn)))
                else:
                    e_j = jnp.where(u == 0, lf + up, jnp.where(u == 1, lf, jnp.where(u == 2, 0.0, dn)))
            else:
                diff = b - _bcast_block_row(b, blk, half - 1 + d)
                e_j = jnp.where(qside, diff, -diff)
            z = (jnp.where(qside, q, k) * jnp.exp2(e_j)).astype(BF16)
            for h in range(H_R):
                cs = slice(h * DK_R, (h + 1) * DK_R)
                p_j = _dot_nt(z[:, cs], z[:, cs])
                a_h[h] = jnp.where(code == j, p_j, 0.0 if a_h[h] is None else a_h[h])
            yield

        edge = b[L - 1:L, :] if d == 0 else b[0:1, :]
        q_in = (q * jnp.exp2(b)).astype(BF16)
        k_out = (k * jnp.exp2(edge - b)).astype(BF16)
        dec = jnp.exp2(edge)
        qk = q * k
        outs = []
        for h in range(H_R):
            cs = slice(h * DK_R, (h + 1) * DK_R)
            st = st_ref[s, d, h]
            o = _dot(a_h[h].astype(BF16), vb[:, cs]) + _dot_nt(q_in[:, cs], st.astype(BF16))
            o = o + jnp.sum(qk[:, cs], axis=-1, keepdims=True) * v[:, cs]
            st_ref[s, d, h] = st * dec[:, cs] + _dot(v[:, cs].T.astype(BF16), k_out[:, cs])
            outs.append(o)
        out_ref[rows, :] = jnp.concatenate(outs, axis=1)

    def step(i, carry):
        s, c = i // n_chunks, i % n_chunks
        _in_turn([direction(0, s, c, of_ref), direction(1, s, n_chunks - 1 - c, ob_ref)])
        return carry

    @pl.when(jnp.max(worst) > FAST_LIMIT)
    def _():
        init_states()
        lax.fori_loop(0, nseq * n_chunks, step, 0)

    def finish(c, carry):
        rows = pl.ds(pl.multiple_of(c * L, L), L)
        o = of_ref[rows, :] + ob_ref[rows, :]
        g = proj_ref[rows, g_cols]
        parts = [_rms(o[:, h * DV_R:(h + 1) * DV_R]) for h in range(H_R)]
        y = jnp.concatenate(parts, axis=1) * gn_ref[...]
        o_ref[rows, :] = y * (g / (1.0 + jnp.exp(-g)))
        return carry

    lax.fori_loop(0, nseq * n_chunks, finish, 0)

    if want_s:
        for s in range(nseq):
            own = s_ref.at[s] if has_prev else s_ref.at[s, layer]
            for d in range(2):
                for h in range(H_R):
                    own[d, h] = st_ref[s, d, h].T
            if not has_prev:
                for other in range(DEPTH):
                    if other != layer:
                        s_ref[s, other] = jnp.zeros((2, H_R, DK_R, DV_R), F32)


def _hgrn(hb, w_in, lb_raw, gn, layer, batch, seq, s0=None, s_prev=None, want_s=False):
    has_s0 = s0 is not None
    nseq = max(1, HGRN_ROWS // seq)
    rows = nseq * seq
    state_spec = pl.BlockSpec((nseq, None, 2, H_R, DK_R, DV_R), lambda b: (b, layer, 0, 0, 0, 0))
    in_specs = [pl.BlockSpec((rows, D_MODEL), lambda b: (b, 0)),
                pl.BlockSpec((None, D_MODEL, IN_R), lambda b: (layer, 0, 0)),
                pl.BlockSpec((DEPTH, 2, W_R), lambda b: (0, 0, 0)),
                _layer_spec(gn, layer)]
    args = [hb, w_in, lb_raw, gn]
    if has_s0:
        in_specs.append(state_spec)
        args.append(s0)
    out_shape = [jax.ShapeDtypeStruct((batch * seq, W_R), F32)]
    out_specs = [pl.BlockSpec((rows, W_R), lambda b: (b, 0))]
    aliases = {}
    if want_s:
        out_shape.append(jax.ShapeDtypeStruct((batch, DEPTH, 2, H_R, DK_R, DV_R), F32))
        if s_prev is not None:
            out_specs.append(state_spec)
            aliases[len(args)] = 1
            in_specs.append(pl.BlockSpec(memory_space=pl.ANY))
            args.append(s_prev)
        else:
            out_specs.append(pl.BlockSpec((nseq, DEPTH, 2, H_R, DK_R, DV_R), lambda b: (b, 0, 0, 0, 0, 0)))
    res = pl.pallas_call(
        functools.partial(_hgrn_kernel, layer=layer, seq=seq, nseq=nseq, has_s0=has_s0,
                          has_prev=bool(aliases), want_s=want_s),
        out_shape=out_shape,
        grid=(batch // nseq,),
        in_specs=in_specs,
        out_specs=out_specs,
        scratch_shapes=[pltpu.VMEM((rows, IN_R), F32),
                        pltpu.VMEM((rows, W_R), F32), pltpu.VMEM((rows, W_R), F32),
                        pltpu.VMEM((nseq, 2, H_R, DV_R, DK_R), F32),
                        pltpu.VMEM((2, HGRN_CHUNK, HGRN_CHUNK), jnp.int32),
                        pltpu.VMEM((2, rows, W_R), F32), pltpu.VMEM((2, rows, W_R), F32)],
        input_output_aliases=aliases,
        compiler_params=_params("arbitrary"),
        name="hgrn",
    )(*args)
    return res if want_s else (res[0], None)


def _head_mean_square(x):
    w = x.shape[1]
    shift = HEAD_DIM.bit_length() - 1
    r = lax.broadcasted_iota(jnp.int32, (w, w), 0) >> shift
    c = lax.broadcasted_iota(jnp.int32, (w, w), 1) >> shift
    ones = jnp.where(r == c, 1.0, 0.0).astype(BF16)
    return _dot((x * x).astype(BF16), ones) * (1.0 / HEAD_DIM)


def _attn_kernel(*refs, seq, nseq, past, n_prev, layer):
    refs = list(refs)
    h_ref, wq0_ref, wq1_ref, wkv_ref, qg_ref, kg_ref = refs[:6]
    pos = 6
    if past:
        ck_ref, cv_ref, cos_ref, sa_ref, sb_ref = refs[pos:pos + 5]
        pos += 5
        o_ref = refs[pos]
        pos += 1
    else:
        pos += n_prev
        o_ref, kn_ref, vn_ref = refs[pos:pos + 3]
        pos += 3
    qs_ref, ks_ref, vt_ref = refs[pos:pos + 3]

    def rope(x):
        up = pltpu.roll(x, LANES - ROPE_AXIS_DIM // 2, axis=1)
        dn = pltpu.roll(x, ROPE_AXIS_DIM // 2, axis=1)
        return x * cos_ref[...] + up * sa_ref[...] + dn * sb_ref[...]

    hb = h_ref[...]
    q = jnp.concatenate([_dot(hb, wq0_ref[...]), _dot(hb, wq1_ref[...])], axis=1)
    kv = _dot(hb, wkv_ref[...])
    k = kv[:, 0:LANES]
    v = kv[:, LANES:2 * LANES]
    qn = q * lax.rsqrt(_head_mean_square(q) + EPS) * qg_ref[...]
    kn = k * lax.rsqrt(_head_mean_square(k) + EPS) * kg_ref[...]
    scale = HEAD_DIM ** -0.5 * LOG2E
    low = lax.broadcasted_iota(jnp.int32, (nseq * seq, LANES), 1) < HEAD_DIM
    pairs_per_kv = (W_A // LANES) // KV_A
    for p in range(W_A // LANES):
        cs = slice(p * LANES, (p + 1) * LANES)
        qp = (rope(qn[:, cs]) if past else qn[:, cs]) * scale
        even = jnp.where(low, qp, 0.0)
        odd = jnp.where(low, 0.0, qp)
        if p // pairs_per_kv == 0:
            odd = pltpu.roll(odd, HEAD_DIM, axis=1)
        else:
            even = pltpu.roll(even, HEAD_DIM, axis=1)
        qs_ref[0, :, cs] = even.astype(BF16)
        qs_ref[1, :, cs] = odd.astype(BF16)
    if past:
        ks_ref[0] = jnp.concatenate([ck_ref[...], rope(kn)], axis=0).astype(BF16)
        vt_ref[0] = jnp.concatenate([cv_ref[...], v], axis=0).T.astype(BF16)
    else:
        for i in range(nseq):
            rs = slice(i * seq, (i + 1) * seq)
            ks_ref[i] = kn[rs, :].astype(BF16)
            vt_ref[i] = v[rs, :].T.astype(BF16)
            if n_prev:
                kn_ref[i] = kn[rs, :]
                vn_ref[i] = v[rs, :]
            else:
                for other in range(DEPTH):
                    kn_ref[i, other] = kn[rs, :] if other == layer else jnp.zeros((seq, LANES), F32)
                    vn_ref[i, other] = v[rs, :] if other == layer else jnp.zeros((seq, LANES), F32)

    tq = min(ATTN_QROWS, seq)
    total = past + seq

    kb = ATTN_KEYS if total % ATTN_KEYS == 0 else total
    n_blocks = total // kb
    n_tiles = seq // tq
    seqs = range(nseq) if n_tiles == 1 else range(1)
    steps = [(i, p, kc) for i in seqs for p in range(W_A // LANES) for kc in range(n_blocks)]

    def tile(r, carry):

        def q_rows(i):
            start = i * seq + r * tq
            return slice(start, start + tq) if isinstance(start, int) else pl.ds(pl.multiple_of(start, tq), tq)

        def scores(step):
            i, p, kc = step
            cs = slice(p * LANES, (p + 1) * LANES)
            q2 = jnp.concatenate([qs_ref[0, q_rows(i), cs], qs_ref[1, q_rows(i), cs]], axis=0)
            return _dot_nt(ks_ref[i, kc * kb:(kc + 1) * kb, :], q2)

        ahead = [scores(st) for st in steps[:ATTN_LOOKAHEAD]]
        for n, (i, p, kc) in enumerate(steps):
            s = ahead.pop(0)
            if n + ATTN_LOOKAHEAD < len(steps):
                ahead.append(scores(steps[n + ATTN_LOOKAHEAD]))
            kh = p // pairs_per_kv
            v_t = vt_ref[i, kh * HEAD_DIM:(kh + 1) * HEAD_DIM, kc * kb:(kc + 1) * kb]
            s_max = jnp.max(s, axis=0, keepdims=True)
            if kc == 0:
                m = s_max
                pexp = jnp.exp2(s - m)
                den = jnp.sum(pexp, axis=0, keepdims=True)
                o2 = _dot(v_t, pexp.astype(BF16))
            else:
                m_new = jnp.maximum(m, s_max)
                alpha = jnp.exp2(m - m_new)
                pexp = jnp.exp2(s - m_new)
                den = den * alpha + jnp.sum(pexp, axis=0, keepdims=True)
                o2 = o2 * alpha + _dot(v_t, pexp.astype(BF16))
                m = m_new
            if kc == n_blocks - 1:
                o2 = o2 / den
                o_ref[q_rows(i), p * LANES:(p + 1) * LANES] = jnp.concatenate([o2[:, 0:tq], o2[:, tq:2 * tq]], axis=0).T
        return carry

    if n_tiles == 1:
        tile(0, 0)
    else:
        assert nseq == 1
        lax.fori_loop(0, n_tiles, tile, 0)


def _attn(hb, w_in, qg, kg, batch, seq, layer, cache=None, rope_tabs=None, kv_prev=None):
    past = 0 if cache is None else cache[0].shape[2]
    wide = 2 * LANES
    w_specs = [pl.BlockSpec((None, D_MODEL, wide), lambda b, j=j: (layer, 0, IN_R // wide + j))
               for j in range(IN_A // wide)]
    nseq = max(1, ATTN_ROWS // seq)
    rows = nseq * seq
    in_specs = [pl.BlockSpec((rows, D_MODEL), lambda b: (b, 0)), *w_specs,
                _layer_spec(qg, layer), _layer_spec(kg, layer)]
    args = [hb, w_in, w_in, w_in, qg, kg]
    out_shape = [jax.ShapeDtypeStruct((batch * seq, W_A), F32)]
    out_specs = [pl.BlockSpec((rows, W_A), lambda b: (b, 0))]
    aliases = {}
    if past:
        ck, cv = cache
        cspec = pl.BlockSpec((None, None, past, LANES), lambda b: (b, layer, 0, 0))
        tspec = pl.BlockSpec((seq, LANES), lambda b: (0, 0))
        in_specs += [cspec, cspec, tspec, tspec, tspec]
        args += [ck, cv, *rope_tabs]
    else:
        for i in range(2):
            out_shape.append(jax.ShapeDtypeStruct((batch, DEPTH, seq, LANES), F32))
            if kv_prev is not None:
                out_specs.append(pl.BlockSpec((nseq, None, seq, LANES), lambda b: (b, layer, 0, 0)))
                aliases[len(args)] = 1 + i
                in_specs.append(pl.BlockSpec(memory_space=pl.ANY))
                args.append(kv_prev[i])
            else:
                out_specs.append(pl.BlockSpec((nseq, DEPTH, seq, LANES), lambda b: (b, 0, 0, 0)))
    res = pl.pallas_call(
        functools.partial(_attn_kernel, seq=seq, nseq=nseq, past=past, n_prev=len(aliases), layer=layer),
        out_shape=out_shape,
        grid=(batch // nseq,),
        in_specs=in_specs,
        out_specs=out_specs,
        scratch_shapes=[pltpu.VMEM((2, rows, W_A), BF16),
                        pltpu.VMEM((nseq, past + seq, LANES), BF16),
                        pltpu.VMEM((nseq, LANES, past + seq), BF16)],
        input_output_aliases=aliases,
        compiler_params=_params("arbitrary"),
        name="attn",
    )(*args)
    return res


def _outffn_kernel(*refs, final):
    refs = list(refs)
    x_ref, or_ref, oa_ref, mod_ref, g2_ref, wout_ref, w1_ref, w2_ref, last_ref = refs[:9]
    pos = 9
    if not final:
        nmod_ref = refs[pos]
        pos += 1
    o_ref = refs[pos]
    gate1 = mod_ref[:, 2 * D_MODEL:3 * D_MODEL]
    shift2 = mod_ref[:, 3 * D_MODEL:4 * D_MODEL]
    scale2 = mod_ref[:, 4 * D_MODEL:5 * D_MODEL]
    gate2 = mod_ref[:, 5 * D_MODEL:6 * D_MODEL]
    hid_ref = refs[-1]

    def rows_part(rs):
        mix = _dot(or_ref[rs, :].astype(BF16), wout_ref[0:W_R, :]) + _dot(oa_ref[rs, :].astype(BF16), wout_ref[W_R:W_R + W_A, :])
        yield
        x1 = x_ref[rs, :] + gate1 * mix
        h2 = _prenorm(x1, shift2, scale2, g2_ref[...])
        for j in range(D_FF // FFN_CHUNK):
            cs = slice(j * FFN_CHUNK, (j + 1) * FFN_CHUNK)
            hid = jnp.maximum(_dot(h2, w1_ref[:, cs]), 0.0)
            hid_ref[rs, cs] = (hid * hid).astype(BF16)
        yield
        x2 = x1 + gate2 * _dot(hid_ref[rs, :], w2_ref[...])
        yield
        if final:
            o_ref[rs, :] = _rms(x2) * last_ref[...]
        else:
            o_ref[rs, :] = x2
            hn_ref = refs[pos + 1]
            hn_ref[rs, :] = _prenorm(x2, nmod_ref[:, 0:D_MODEL], nmod_ref[:, D_MODEL:2 * D_MODEL], last_ref[...])

    n_rows = x_ref.shape[0]
    part = n_rows // FFN_PARTS
    _in_turn([rows_part(slice(i * part, (i + 1) * part)) for i in range(FFN_PARTS)])


def _outffn(x, o_r, o_a, mod, g1, g2, wout, w1, w2, gf, layer, rows_per_seq, ctx):
    n = x.shape[0]
    tm = ROW_TILE
    final = layer == DEPTH - 1
    row = lambda w: pl.BlockSpec((tm, w), lambda i: (i, 0))
    in_specs = [row(D_MODEL), row(W_R), row(W_A),
                _mod_spec(layer, rows_per_seq, tm, ctx),
                _layer_spec(g2, layer), _layer_spec(wout, layer), _layer_spec(w1, layer), _layer_spec(w2, layer)]
    args = [x, o_r, o_a, mod, g2, wout, w1, w2]
    out_shape = [jax.ShapeDtypeStruct((n, D_MODEL), F32)]
    out_specs = [row(D_MODEL)]
    if final:
        in_specs.append(pl.BlockSpec(gf.shape, lambda i: (0, 0)))
        args.append(gf)
    else:
        in_specs += [_layer_spec(g1, layer + 1), _mod_spec(layer + 1, rows_per_seq, tm, ctx)]
        args += [g1, mod]
        out_shape.append(jax.ShapeDtypeStruct((n, D_MODEL), BF16))
        out_specs.append(row(D_MODEL))
    res = pl.pallas_call(
        functools.partial(_outffn_kernel, final=final),
        out_shape=out_shape,
        grid=(n // tm,),
        in_specs=in_specs,
        out_specs=out_specs,
        scratch_shapes=[pltpu.VMEM((tm, D_FF), BF16)],
        compiler_params=_params("arbitrary"),
        name="outffn",
    )(*args)
    return (res[0], None) if final else res


def _rope_tables(seq):
    rows = seq // GRID_W
    rowi = jnp.repeat(jnp.arange(rows, dtype=F32), GRID_W)
    coli = jnp.tile(jnp.arange(GRID_W, dtype=F32), rows)
    inv = ROPE_THETA ** (-jnp.arange(0, ROPE_AXIS_DIM, 2, dtype=F32) / ROPE_AXIS_DIM)
    ar = rowi[:, None] * inv[None, :]
    ac = coli[:, None] * inv[None, :]
    ang = jnp.concatenate([ar, ar, ac, ac], axis=-1)
    cos = jnp.tile(jnp.cos(ang), (1, LANES // HEAD_DIM))
    sin = jnp.tile(jnp.sin(ang), (1, LANES // HEAD_DIM))
    first = (jnp.arange(LANES) % ROPE_AXIS_DIM) < ROPE_AXIS_DIM // 2
    return cos, jnp.where(first, -sin, 0.0), jnp.where(first, 0.0, sin)


def kernel(x_prompt, x_sample, cache_k, cache_v, state_hgrn, c, c_ctx, w_mod, b_mod, norm1_g, w_in, lb_raw,
           hgrn_norm_g, q_norm_g, k_norm_g, w_out, norm2_g, w1, w2, final_norm_g):
    bp, sp, _ = x_prompt.shape
    bs, ss, _ = x_sample.shape
    past = cache_k.shape[2]

    cvec = jnp.zeros((MOD_ROWS, D_MODEL), F32).at[0:bs].set(c).at[MOD_ROWS // 2].set(c_ctx)
    mod = _modulation(cvec, w_mod, b_mod).reshape(DEPTH, MOD_ROWS, 1, MOD_W)

    w_in_b, w_out_b, w1_b, w2_b = (w.astype(BF16) for w in (w_in, w_out, w1, w2))
    ck = cache_k.reshape(bs, DEPTH, past, KV_A * HEAD_DIM)
    cv = cache_v.reshape(bs, DEPTH, past, KV_A * HEAD_DIM)
    tabs = _rope_tables(ss)
    gf = final_norm_g.reshape(1, D_MODEL)
    g1 = norm1_g.reshape(DEPTH, 1, D_MODEL)
    g2 = norm2_g.reshape(DEPTH, 1, D_MODEL)
    gn = jnp.tile(hgrn_norm_g, (1, H_R)).reshape(DEPTH, 1, W_R)
    qg = jnp.tile(q_norm_g, (1, H_A)).reshape(DEPTH, 1, W_A)
    kg = jnp.tile(k_norm_g, (1, KV_A)).reshape(DEPTH, 1, LANES)

    xp = x_prompt.reshape(bp * sp, D_MODEL)
    xs = x_sample.reshape(bs * ss, D_MODEL)
    hp = _first_prenorm(xp, mod, g1, sp, True)
    hs = _first_prenorm(xs, mod, g1, ss, False)
    new_s, new_kv = None, None
    for l in range(DEPTH):
        o_r, new_s = _hgrn(hp, w_in_b, lb_raw, gn, l, bp, sp, s_prev=new_s, want_s=True)
        o_a, *new_kv = _attn(hp, w_in_b, qg, kg, bp, sp, l, kv_prev=new_kv)
        xp, hp = _outffn(xp, o_r, o_a, mod, g1, g2, w_out_b, w1_b, w2_b, gf, l, sp, True)

        o_r, _ = _hgrn(hs, w_in_b, lb_raw, gn, l, bs, ss, s0=state_hgrn)
        o_a, = _attn(hs, w_in_b, qg, kg, bs, ss, l, cache=(ck, cv), rope_tabs=tabs)
        xs, hs = _outffn(xs, o_r, o_a, mod, g1, g2, w_out_b, w1_b, w2_b, gf, l, ss, False)

    y_prompt = xp.reshape(bp, sp, D_MODEL)
    y_sample = xs.reshape(bs, ss, D_MODEL)
    new_k = new_kv[0].reshape(bp, DEPTH, sp, KV_A, HEAD_DIM)
    new_v = new_kv[1].reshape(bp, DEPTH, sp, KV_A, HEAD_DIM)
    return (y_prompt, y_sample, new_k, new_v, new_s)
```

```python
import functools

import jax
import jax.numpy as jnp
import numpy as np
from jax import lax
from jax.experimental import pallas as pl
from jax.experimental.pallas import tpu as pltpu

F32 = jnp.float32
BF16 = jnp.bfloat16

D_MODEL = 1024
DEPTH = 4
GRID_W = 64
D_FF = 4 * D_MODEL
H_R = 4
DK_R = 128
DV_R = 128
W_R = H_R * DV_R
H_A = 8
KV_A = 2
HEAD_DIM = 64
W_A = H_A * HEAD_DIM
ROPE_AXIS_DIM = HEAD_DIM // 2
ROPE_THETA = 10000.0
EPS = 1e-6
LOG2E = 1.4426950408889634
IN_R = 3 * H_R * DK_R + 2 * W_R
IN_A = W_A + 2 * KV_A * HEAD_DIM
MOD_W = 6 * D_MODEL
MOD_ROWS = 8

LANES = 128
SUBLANES = 8
VMEM_LIMIT = 56 * 1024 * 1024

HGRN_CHUNK = 128
FAST_CHUNK = 64
FAST_UNROLL = 8
FAST_LIMIT = 115.0
PROJ_ROWS = 256
HGRN_ROWS = 512
ATTN_QROWS = 256
ATTN_KEYS = 256
ATTN_ROWS = 1024
ATTN_LOOKAHEAD = 2
FFN_CHUNK = 1024
FFN_PARTS = 2
ROW_TILE = 512


def _dot(a, b):
    return jnp.dot(a, b, preferred_element_type=F32)


def _dot_nt(a, b):
    return lax.dot_general(a, b, (((1,), (1,)), ((), ())), preferred_element_type=F32)


def _in_turn(gens):
    active = list(gens)
    while active:
        for gen in list(active):
            if next(gen, "done") == "done":
                active.remove(gen)


def _params(*sem):
    return pltpu.CompilerParams(dimension_semantics=sem, vmem_limit_bytes=VMEM_LIMIT)


def _rms(x):
    return x * lax.rsqrt(jnp.mean(x * x, axis=-1, keepdims=True) + EPS)


def _mod_spec(layer, rows_per_seq, tile, ctx):
    per = rows_per_seq // tile
    row = (lambda i: MOD_ROWS // 2) if ctx else (lambda i: i // per)
    return pl.BlockSpec((None, None, 1, MOD_W), lambda i: (layer, row(i), 0, 0))


def _layer_spec(a, layer):
    return pl.BlockSpec((None,) + a.shape[1:], lambda i: (layer, 0, 0))


def _prenorm(x, shift, scale, g):
    return (_rms(x) * g * (1.0 + scale) + shift).astype(BF16)


def _mod_kernel(c_ref, w_ref, b_ref, o_ref):
    c = c_ref[...]
    s = c / (1.0 + jnp.exp(-c))
    o_ref[...] = _dot(s.astype(BF16), w_ref[...].astype(BF16)) + b_ref[...]


def _modulation(cvec, w_mod, b_mod):
    tn = 1536
    return pl.pallas_call(
        _mod_kernel,
        out_shape=jax.ShapeDtypeStruct((DEPTH, MOD_ROWS, MOD_W), F32),
        grid=(DEPTH, MOD_W // tn),
        in_specs=[
            pl.BlockSpec((MOD_ROWS, D_MODEL), lambda l, j: (0, 0)),
            pl.BlockSpec((None, D_MODEL, tn), lambda l, j: (l, 0, j)),
            pl.BlockSpec((None, 1, tn), lambda l, j: (l, 0, j)),
        ],
        out_specs=pl.BlockSpec((None, MOD_ROWS, tn), lambda l, j: (l, 0, j)),
        compiler_params=_params("arbitrary", "arbitrary"),
        name="modulation",
    )(cvec, w_mod, b_mod.reshape(DEPTH, 1, MOD_W))


def _prenorm_kernel(x_ref, mod_ref, g_ref, o_ref):
    o_ref[...] = _prenorm(x_ref[...], mod_ref[:, 0:D_MODEL], mod_ref[:, D_MODEL:2 * D_MODEL], g_ref[...])


def _first_prenorm(x, mod, g1, rows_per_seq, ctx):
    n = x.shape[0]
    tm = ROW_TILE
    return pl.pallas_call(
        _prenorm_kernel,
        out_shape=jax.ShapeDtypeStruct((n, D_MODEL), BF16),
        grid=(n // tm,),
        in_specs=[pl.BlockSpec((tm, D_MODEL), lambda i: (i, 0)),
                  _mod_spec(0, rows_per_seq, tm, ctx),
                  _layer_spec(g1, 0)],
        out_specs=pl.BlockSpec((tm, D_MODEL), lambda i: (i, 0)),
        compiler_params=_params("arbitrary"),
        name="prenorm",
    )(x, mod, g1)


def _bcast_block_row(b, blk, r):
    n, w = b.shape
    parts = [jnp.broadcast_to(b[k * blk + r:k * blk + r + 1, :], (blk, w)) for k in range(n // blk)]
    return parts[0] if len(parts) == 1 else jnp.concatenate(parts, axis=0)


def _hgrn_kernel(*refs, layer, seq, nseq, has_s0, has_prev, want_s):
    L = HGRN_CHUNK
    n_chunks = seq // L
    n_levels = int(np.log2(L))
    refs = list(refs)
    h_ref, w_ref, lbraw_ref, gn_ref = refs[:4]
    pos = 4
    s0_ref = None
    if has_s0:
        s0_ref = refs[pos]
        pos += 1
    if has_prev:
        pos += 1
    o_ref = refs[pos]
    pos += 1
    s_ref = None
    if want_s:
        s_ref = refs[pos]
        pos += 1
    proj_ref, of_ref, ob_ref, st_ref, code_ref, kg_ref, b_ref = refs[pos:pos + 7]
    q_cols, i_cols, g_cols = (slice(n * W_R, (n + 1) * W_R) for n in (0, 3, 4))
    f_cols = [slice(W_R, 2 * W_R), slice(2 * W_R, 3 * W_R)]

    def project(g):
        rows = slice(g * PROJ_ROWS, (g + 1) * PROJ_ROWS)
        proj_ref[rows, :] = _dot(h_ref[rows, :], w_ref[...])

    if layer > 0:
        raw = lbraw_ref[...]
        e = jnp.exp(raw - jnp.max(raw, axis=0, keepdims=True))
        p = e / jnp.sum(e, axis=0, keepdims=True)
        lb = p[1]
        for j in range(2, layer + 1):
            lb = lb + p[j]
        log_lb = jnp.log(lb)
        log_1m_lb = jnp.log1p(-lb)
        one_m_lb = 1.0 - lb

    def gates(fx, d):
        ls = jnp.minimum(fx, 0.0) - jnp.log(1.0 + jnp.exp(-jnp.abs(fx)))
        k = jnp.exp(ls - fx)
        if layer == 0:
            return ls, k
        a = log_lb[d:d + 1, :]
        c = log_1m_lb[d:d + 1, :] + ls
        lf = jnp.maximum(a, c) + jnp.log(1.0 + jnp.exp(-jnp.abs(a - c)))
        return lf, one_m_lb[d:d + 1, :] * k

    ti = lax.broadcasted_iota(jnp.int32, (L, L), 0)
    si = lax.broadcasted_iota(jnp.int32, (L, L), 1)
    x = ti ^ si
    lv = jnp.zeros((L, L), jnp.int32)
    for j in range(n_levels):
        lv = lv + jnp.where((x >> j) != 0, 1, 0)
    code_ref[0] = jnp.where(ti >= si, lv, -1)
    code_ref[1] = jnp.where(ti <= si, lv, -1)

    def init_states():
        for s in range(nseq):
            for d in range(2):
                for h in range(H_R):
                    if has_s0:
                        st_ref[s, d, h] = s0_ref[s, d, h].T
                    else:
                        st_ref[s, d, h] = jnp.zeros((DV_R, DK_R), F32)

    init_states()

    def cumulative(lf, tri):
        hi = lf.astype(BF16)
        r1 = lf - hi.astype(F32)
        mid = r1.astype(BF16)
        lo = (r1 - mid.astype(F32)).astype(BF16)
        b3 = _dot(tri, jnp.concatenate([hi, mid, lo], axis=1))
        return b3[:, 0:W_R] + b3[:, W_R:2 * W_R] + b3[:, 2 * W_R:3 * W_R]

    Lf = FAST_CHUNK
    n_fast = seq // Lf
    unroll = min(FAST_UNROLL, n_fast)
    ref_row = [Lf // 2 - 1, Lf // 2]
    first_row = [0, Lf - 1]
    last_row = [Lf - 1, 0]
    tf = lax.broadcasted_iota(jnp.int32, (Lf, Lf), 0)
    sf = lax.broadcasted_iota(jnp.int32, (Lf, Lf), 1)
    keep = [tf >= sf, tf <= sf]
    tf2 = lax.broadcasted_iota(jnp.int32, (Lf, 2 * Lf), 0)
    sf2 = lax.broadcasted_iota(jnp.int32, (Lf, 2 * Lf), 1) & (Lf - 1)
    keep2 = [tf2 >= sf2, tf2 <= sf2]

    def prepare(g, worst):
        for i in range(PROJ_ROWS // Lf):
            rows = slice(g * PROJ_ROWS + i * Lf, g * PROJ_ROWS + (i + 1) * Lf)
            q_top = jnp.max(jnp.abs(proj_ref[rows, q_cols]), axis=0, keepdims=True)
            q_bits = jnp.log(jnp.maximum(q_top, 1.0)) * LOG2E
            for d in range(2):
                lf, k = gates(proj_ref[rows, f_cols[d]], d)
                b = cumulative(lf * LOG2E, jnp.where(keep[d], 1.0, 0.0).astype(BF16))
                kg_ref[d, rows, :] = k
                b_ref[d, rows, :] = b
                b_mid = b[ref_row[d]:ref_row[d] + 1, :]
                grow = jnp.maximum(b[first_row[d]:first_row[d] + 1, :] - b_mid,
                                   b_mid - b[last_row[d]:last_row[d] + 1, :])
                worst = jnp.maximum(worst, grow + q_bits)
        return worst

    def fast_direction(d, s, c, out_ref):
        start = s * seq + c * Lf
        rows = slice(start, start + Lf) if isinstance(start, int) else pl.ds(pl.multiple_of(start, Lf), Lf)
        q = proj_ref[rows, q_cols]
        v = proj_ref[rows, i_cols]
        k = kg_ref[d, rows, :]
        b = b_ref[d, rows, :]
        b_mid = b[ref_row[d]:ref_row[d] + 1, :]
        edge = b[last_row[d]:last_row[d] + 1, :]
        q_t = (q * jnp.exp2(b - b_mid)).astype(BF16)
        k_f = k * jnp.exp2(b_mid - b)
        k_t = k_f.astype(BF16)
        k_out = (k_f * jnp.exp2(edge - b_mid)).astype(BF16)
        dec = jnp.exp2(edge)
        from_start = jnp.exp2(b_mid)
        vb = v.astype(BF16)
        zero_s = jnp.zeros((DV_R, DK_R), BF16)
        zero_r = jnp.zeros((Lf, DK_R), BF16)

        def side_by_side(x1, x2, zero):
            return jnp.concatenate([jnp.concatenate([x1, zero], axis=1),
                                    jnp.concatenate([zero, x2], axis=1)], axis=0)

        both = []
        for h in range(0, H_R, 2):
            c1 = slice(h * DK_R, (h + 1) * DK_R)
            c2 = slice((h + 1) * DK_R, (h + 2) * DK_R)
            cp = slice(h * DK_R, (h + 2) * DK_R)
            st1 = st_ref[s, d, h]
            st2 = st_ref[s, d, h + 1]
            rhs = jnp.concatenate([
                side_by_side((st1 * from_start[:, c1]).astype(BF16), (st2 * from_start[:, c2]).astype(BF16), zero_s),
                side_by_side(k_t[:, c1], k_t[:, c2], zero_r)], axis=0)
            both.append(_dot_nt(q_t[:, cp], rhs))
            v_rows = jnp.concatenate([v[:, c1], v[:, c2]], axis=0)
            upd = _dot(v_rows.T.astype(BF16), side_by_side(k_out[:, c1], k_out[:, c2], zero_r))
            st_ref[s, d, h] = st1 * dec[:, c1] + upd[:, 0:DK_R]
            st_ref[s, d, h + 1] = st2 * dec[:, c2] + upd[:, DK_R:2 * DK_R]
        yield
        outs = []
        for n, h in enumerate(range(0, H_R, 2)):
            c1 = slice(h * DK_R, (h + 1) * DK_R)
            c2 = slice((h + 1) * DK_R, (h + 2) * DK_R)
            a = jnp.where(keep2[d], both[n][:, 2 * DV_R:2 * DV_R + 2 * Lf], 0.0).astype(BF16)
            outs.append(both[n][:, 0:2 * DV_R] + _dot(a, side_by_side(vb[:, c1], vb[:, c2], zero_r)))
        out_ref[rows, :] = jnp.concatenate(outs, axis=1)

    def fast_chunks(s, first):
        gens = []
        for i in range(unroll):
            gens.append(fast_direction(0, s, first + i, of_ref))
            gens.append(fast_direction(1, s, n_fast - 1 - (first + i), ob_ref))
        _in_turn(gens)

    n_groups = nseq * seq // PROJ_ROWS
    inline = n_fast == unroll
    worst = jnp.zeros((1, W_R), F32)
    project(0)
    for g in range(n_groups):
        if g + 1 < n_groups:
            project(g + 1)
        worst = prepare(g, worst)
        if inline and ((g + 1) * PROJ_ROWS) % seq == 0:
            fast_chunks((g + 1) * PROJ_ROWS // seq - 1, 0)
    if not inline:
        for s in range(nseq):
            def fast_step(c, carry, s=s):
                fast_chunks(s, c * unroll)
                return carry
            lax.fori_loop(0, n_fast // unroll, fast_step, 0)

    def direction(d, s, c, out_ref):
        rows = pl.ds(pl.multiple_of(s * seq + c * L, L), L)
        row = lax.broadcasted_iota(jnp.int32, (L, W_R), 0)
        code = code_ref[d]
        tri = jnp.where(code >= 0, 1.0, 0.0).astype(BF16)
        q = proj_ref[rows, q_cols]
        v = proj_ref[rows, i_cols]
        lf, k = gates(proj_ref[rows, f_cols[d]], d)
        lf = lf * LOG2E
        b = cumulative(lf, tri)
        yield

        vb = v.astype(BF16)
        a_h = [None] * H_R
        for j in range(1, n_levels + 1):
            half = 1 << (j - 1)
            blk = 2 * half
            u = row & (blk - 1)
            qside = (u >= half) if d == 0 else (u < half)
            if j == 1:
                e_j = jnp.where(qside, lf, 0.0)
            elif j == 2:
                up = pltpu.roll(lf, L - 1, axis=0)
                dn = pltpu.roll(lf, 1, axis=0)
                if d == 0:
                    e_j = jnp.where(u == 0, up, jnp.where(u == 1, 0.0, jnp.where(u == 2, lf, lf + dn)))
                else:
                    e_j = jnp.where(u == 0, lf + up, jnp.where(u == 1, lf, jnp.where(u == 2, 0.0, dn)))
            else:
                diff = b - _bcast_block_row(b, blk, half - 1 + d)
                e_j = jnp.where(qside, diff, -diff)
            z = (jnp.where(qside, q, k) * jnp.exp2(e_j)).astype(BF16)
            for h in range(H_R):
                cs = slice(h * DK_R, (h + 1) * DK_R)
                p_j = _dot_nt(z[:, cs], z[:, cs])
                a_h[h] = jnp.where(code == j, p_j, 0.0 if a_h[h] is None else a_h[h])
            yield

        edge = b[L - 1:L, :] if d == 0 else b[0:1, :]
        q_in = (q * jnp.exp2(b)).astype(BF16)
        k_out = (k * jnp.exp2(edge - b)).astype(BF16)
        dec = jnp.exp2(edge)
        qk = q * k
        outs = []
        for h in range(H_R):
            cs = slice(h * DK_R, (h + 1) * DK_R)
            st = st_ref[s, d, h]
            o = _dot(a_h[h].astype(BF16), vb[:, cs]) + _dot_nt(q_in[:, cs], st.astype(BF16))
            o = o + jnp.sum(qk[:, cs], axis=-1, keepdims=True) * v[:, cs]
            st_ref[s, d, h] = st * dec[:, cs] + _dot(v[:, cs].T.astype(BF16), k_out[:, cs])
            outs.append(o)
        out_ref[rows, :] = jnp.concatenate(outs, axis=1)

    def step(i, carry):
        s, c = i // n_chunks, i % n_chunks
        _in_turn([direction(0, s, c, of_ref), direction(1, s, n_chunks - 1 - c, ob_ref)])
        return carry

    @pl.when(jnp.max(worst) > FAST_LIMIT)
    def _():
        init_states()
        lax.fori_loop(0, nseq * n_chunks, step, 0)

    def finish(c, carry):
        rows = pl.ds(pl.multiple_of(c * L, L), L)
        o = of_ref[rows, :] + ob_ref[rows, :]
        g = proj_ref[rows, g_cols]
        parts = [_rms(o[:, h * DV_R:(h + 1) * DV_R]) for h in range(H_R)]
        y = jnp.concatenate(parts, axis=1) * gn_ref[...]
        o_ref[rows, :] = y * (g / (1.0 + jnp.exp(-g)))
        return carry

    lax.fori_loop(0, nseq * n_chunks, finish, 0)

    if want_s:
        for s in range(nseq):
            own = s_ref.at[s] if has_prev else s_ref.at[s, layer]
            for d in range(2):
                for h in range(H_R):
                    own[d, h] = st_ref[s, d, h].T
            if not has_prev:
                for other in range(DEPTH):
                    if other != layer:
                        s_ref[s, other] = jnp.zeros((2, H_R, DK_R, DV_R), F32)


def _hgrn(hb, w_in, lb_raw, gn, layer, batch, seq, s0=None, s_prev=None, want_s=False):
    has_s0 = s0 is not None
    nseq = max(1, HGRN_ROWS // seq)
    rows = nseq * seq
    state_spec = pl.BlockSpec((nseq, None, 2, H_R, DK_R, DV_R), lambda b: (b, layer, 0, 0, 0, 0))
    in_specs = [pl.BlockSpec((rows, D_MODEL), lambda b: (b, 0)),
                pl.BlockSpec((D_MODEL, IN_R), lambda b: (0, 0)),
                pl.BlockSpec((DEPTH, 2, W_R), lambda b: (0, 0, 0)),
                _layer_spec(gn, layer)]
    args = [hb, w_in, lb_raw, gn]
    if has_s0:
        in_specs.append(state_spec)
        args.append(s0)
    out_shape = [jax.ShapeDtypeStruct((batch * seq, W_R), F32)]
    out_specs = [pl.BlockSpec((rows, W_R), lambda b: (b, 0))]
    aliases = {}
    if want_s:
        out_shape.append(jax.ShapeDtypeStruct((batch, DEPTH, 2, H_R, DK_R, DV_R), F32))
        if s_prev is not None:
            out_specs.append(state_spec)
            aliases[len(args)] = 1
            in_specs.append(pl.BlockSpec(memory_space=pl.ANY))
            args.append(s_prev)
        else:
            out_specs.append(pl.BlockSpec((nseq, DEPTH, 2, H_R, DK_R, DV_R), lambda b: (b, 0, 0, 0, 0, 0)))
    res = pl.pallas_call(
        functools.partial(_hgrn_kernel, layer=layer, seq=seq, nseq=nseq, has_s0=has_s0,
                          has_prev=bool(aliases), want_s=want_s),
        out_shape=out_shape,
        grid=(batch // nseq,),
        in_specs=in_specs,
        out_specs=out_specs,
        scratch_shapes=[pltpu.VMEM((rows, IN_R), F32),
                        pltpu.VMEM((rows, W_R), F32), pltpu.VMEM((rows, W_R), F32),
                        pltpu.VMEM((nseq, 2, H_R, DV_R, DK_R), F32),
                        pltpu.VMEM((2, HGRN_CHUNK, HGRN_CHUNK), jnp.int32),
                        pltpu.VMEM((2, rows, W_R), F32), pltpu.VMEM((2, rows, W_R), F32)],
        input_output_aliases=aliases,
        compiler_params=_params("arbitrary"),
        name="hgrn",
    )(*args)
    return res if want_s else (res[0], None)


def _head_mean_square(x):
    w = x.shape[1]
    shift = HEAD_DIM.bit_length() - 1
    r = lax.broadcasted_iota(jnp.int32, (w, w), 0) >> shift
    c = lax.broadcasted_iota(jnp.int32, (w, w), 1) >> shift
    ones = jnp.where(r == c, 1.0, 0.0).astype(BF16)
    return _dot((x * x).astype(BF16), ones) * (1.0 / HEAD_DIM)


def _attn_kernel(*refs, seq, nseq, past, n_prev, layer):
    refs = list(refs)
    h_ref, wq0_ref, wq1_ref, wkv_ref, qg_ref, kg_ref = refs[:6]
    pos = 6
    if past:
        ck_ref, cv_ref, cos_ref, sa_ref, sb_ref = refs[pos:pos + 5]
        pos += 5
        o_ref = refs[pos]
        pos += 1
    else:
        pos += n_prev
        o_ref, kn_ref, vn_ref = refs[pos:pos + 3]
        pos += 3
    qs_ref, ks_ref, vt_ref = refs[pos:pos + 3]

    def rope(x):
        up = pltpu.roll(x, LANES - ROPE_AXIS_DIM // 2, axis=1)
        dn = pltpu.roll(x, ROPE_AXIS_DIM // 2, axis=1)
        return x * cos_ref[...] + up * sa_ref[...] + dn * sb_ref[...]

    hb = h_ref[...]
    q = jnp.concatenate([_dot(hb, wq0_ref[...]), _dot(hb, wq1_ref[...])], axis=1)
    kv = _dot(hb, wkv_ref[...])
    k = kv[:, 0:LANES]
    v = kv[:, LANES:2 * LANES]
    qn = q * lax.rsqrt(_head_mean_square(q) + EPS) * qg_ref[...]
    kn = k * lax.rsqrt(_head_mean_square(k) + EPS) * kg_ref[...]
    scale = HEAD_DIM ** -0.5 * LOG2E
    low = lax.broadcasted_iota(jnp.int32, (nseq * seq, LANES), 1) < HEAD_DIM
    pairs_per_kv = (W_A // LANES) // KV_A
    for p in range(W_A // LANES):
        cs = slice(p * LANES, (p + 1) * LANES)
        qp = (rope(qn[:, cs]) if past else qn[:, cs]) * scale
        even = jnp.where(low, qp, 0.0)
        odd = jnp.where(low, 0.0, qp)
        if p // pairs_per_kv == 0:
            odd = pltpu.roll(odd, HEAD_DIM, axis=1)
        else:
            even = pltpu.roll(even, HEAD_DIM, axis=1)
        qs_ref[0, :, cs] = even.astype(BF16)
        qs_ref[1, :, cs] = odd.astype(BF16)
    if past:
        ks_ref[0] = jnp.concatenate([ck_ref[...], rope(kn)], axis=0).astype(BF16)
        vt_ref[0] = jnp.concatenate([cv_ref[...], v], axis=0).T.astype(BF16)
    else:
        for i in range(nseq):
            rs = slice(i * seq, (i + 1) * seq)
            ks_ref[i] = kn[rs, :].astype(BF16)
            vt_ref[i] = v[rs, :].T.astype(BF16)
            if n_prev:
                kn_ref[i] = kn[rs, :]
                vn_ref[i] = v[rs, :]
            else:
                for other in range(DEPTH):
                    kn_ref[i, other] = kn[rs, :] if other == layer else jnp.zeros((seq, LANES), F32)
                    vn_ref[i, other] = v[rs, :] if other == layer else jnp.zeros((seq, LANES), F32)

    tq = min(ATTN_QROWS, seq)
    total = past + seq

    kb = ATTN_KEYS if total % ATTN_KEYS == 0 else total
    n_blocks = total // kb
    n_tiles = seq // tq
    seqs = range(nseq) if n_tiles == 1 else range(1)
    steps = [(i, p, kc) for i in seqs for p in range(W_A // LANES) for kc in range(n_blocks)]

    def tile(r, carry):

        def q_rows(i):
            start = i * seq + r * tq
            return slice(start, start + tq) if isinstance(start, int) else pl.ds(pl.multiple_of(start, tq), tq)

        def scores(step):
            i, p, kc = step
            cs = slice(p * LANES, (p + 1) * LANES)
            q2 = jnp.concatenate([qs_ref[0, q_rows(i), cs], qs_ref[1, q_rows(i), cs]], axis=0)
            return _dot_nt(ks_ref[i, kc * kb:(kc + 1) * kb, :], q2)

        ahead = [scores(st) for st in steps[:ATTN_LOOKAHEAD]]
        for n, (i, p, kc) in enumerate(steps):
            s = ahead.pop(0)
            if n + ATTN_LOOKAHEAD < len(steps):
                ahead.append(scores(steps[n + ATTN_LOOKAHEAD]))
            kh = p // pairs_per_kv
            v_t = vt_ref[i, kh * HEAD_DIM:(kh + 1) * HEAD_DIM, kc * kb:(kc + 1) * kb]
            s_max = jnp.max(s, axis=0, keepdims=True)
            if kc == 0:
                m = s_max
                pexp = jnp.exp2(s - m)
                den = jnp.sum(pexp, axis=0, keepdims=True)
                o2 = _dot(v_t, pexp.astype(BF16))
            else:
                m_new = jnp.maximum(m, s_max)
                alpha = jnp.exp2(m - m_new)
                pexp = jnp.exp2(s - m_new)
                den = den * alpha + jnp.sum(pexp, axis=0, keepdims=True)
                o2 = o2 * alpha + _dot(v_t, pexp.astype(BF16))
                m = m_new
            if kc == n_blocks - 1:
                o2 = o2 / den
                o_ref[q_rows(i), p * LANES:(p + 1) * LANES] = jnp.concatenate([o2[:, 0:tq], o2[:, tq:2 * tq]], axis=0).T
        return carry

    if n_tiles == 1:
        tile(0, 0)
    else:
        assert nseq == 1
        lax.fori_loop(0, n_tiles, tile, 0)


def _attn(hb, w_in, qg, kg, batch, seq, layer, cache=None, rope_tabs=None, kv_prev=None):
    past = 0 if cache is None else cache[0].shape[2]
    wide = 2 * LANES
    w_specs = [pl.BlockSpec((D_MODEL, wide), lambda b, j=j: (0, IN_R // wide + j))
               for j in range(IN_A // wide)]
    nseq = max(1, ATTN_ROWS // seq)
    rows = nseq * seq
    in_specs = [pl.BlockSpec((rows, D_MODEL), lambda b: (b, 0)), *w_specs,
                _layer_spec(qg, layer), _layer_spec(kg, layer)]
    args = [hb, w_in, w_in, w_in, qg, kg]
    out_shape = [jax.ShapeDtypeStruct((batch * seq, W_A), F32)]
    out_specs = [pl.BlockSpec((rows, W_A), lambda b: (b, 0))]
    aliases = {}
    if past:
        ck, cv = cache
        cspec = pl.BlockSpec((None, None, past, LANES), lambda b: (b, layer, 0, 0))
        tspec = pl.BlockSpec((seq, LANES), lambda b: (0, 0))
        in_specs += [cspec, cspec, tspec, tspec, tspec]
        args += [ck, cv, *rope_tabs]
    else:
        for i in range(2):
            out_shape.append(jax.ShapeDtypeStruct((batch, DEPTH, seq, LANES), F32))
            if kv_prev is not None:
                out_specs.append(pl.BlockSpec((nseq, None, seq, LANES), lambda b: (b, layer, 0, 0)))
                aliases[len(args)] = 1 + i
                in_specs.append(pl.BlockSpec(memory_space=pl.ANY))
                args.append(kv_prev[i])
            else:
                out_specs.append(pl.BlockSpec((nseq, DEPTH, seq, LANES), lambda b: (b, 0, 0, 0)))
    res = pl.pallas_call(
        functools.partial(_attn_kernel, seq=seq, nseq=nseq, past=past, n_prev=len(aliases), layer=layer),
        out_shape=out_shape,
        grid=(batch // nseq,),
        in_specs=in_specs,
        out_specs=out_specs,
        scratch_shapes=[pltpu.VMEM((2, rows, W_A), BF16),
                        pltpu.VMEM((nseq, past + seq, LANES), BF16),
                        pltpu.VMEM((nseq, LANES, past + seq), BF16)],
        input_output_aliases=aliases,
        compiler_params=_params("arbitrary"),
        name="attn",
    )(*args)
    return res


def _outffn_kernel(*refs, final, n_cast):
    refs = list(refs)
    x_ref, or_ref, oa_ref, mod_ref, g2_ref, wout_ref, w1_ref, w2_ref, last_ref = refs[:9]
    pos = 9
    if not final:
        nmod_ref = refs[pos]
        pos += 1
    n_out = 1 if final else 2
    for src, dst in zip(refs[pos:pos + n_cast], refs[pos + n_cast + n_out:pos + 2 * n_cast + n_out]):
        dst[...] = src[...].astype(BF16)
    pos += n_cast
    o_ref = refs[pos]
    gate1 = mod_ref[:, 2 * D_MODEL:3 * D_MODEL]
    shift2 = mod_ref[:, 3 * D_MODEL:4 * D_MODEL]
    scale2 = mod_ref[:, 4 * D_MODEL:5 * D_MODEL]
    gate2 = mod_ref[:, 5 * D_MODEL:6 * D_MODEL]
    hid_ref = refs[-1]

    def rows_part(rs):
        mix = _dot(or_ref[rs, :].astype(BF16), wout_ref[0:W_R, :]) + _dot(oa_ref[rs, :].astype(BF16), wout_ref[W_R:W_R + W_A, :])
        yield
        x1 = x_ref[rs, :] + gate1 * mix
        h2 = _prenorm(x1, shift2, scale2, g2_ref[...])
        for j in range(D_FF // FFN_CHUNK):
            cs = slice(j * FFN_CHUNK, (j + 1) * FFN_CHUNK)
            hid = jnp.maximum(_dot(h2, w1_ref[:, cs]), 0.0)
            hid_ref[rs, cs] = (hid * hid).astype(BF16)
        yield
        x2 = x1 + gate2 * _dot(hid_ref[rs, :], w2_ref[...])
        yield
        if final:
            o_ref[rs, :] = _rms(x2) * last_ref[...]
        else:
            o_ref[rs, :] = x2
            hn_ref = refs[pos + 1]
            hn_ref[rs, :] = _prenorm(x2, nmod_ref[:, 0:D_MODEL], nmod_ref[:, D_MODEL:2 * D_MODEL], last_ref[...])

    n_rows = x_ref.shape[0]
    part = n_rows // FFN_PARTS
    _in_turn([rows_part(slice(i * part, (i + 1) * part)) for i in range(FFN_PARTS)])


def _outffn(x, o_r, o_a, mod, g1, g2, wout, w1, w2, gf, layer, rows_per_seq, ctx, cast=()):
    n = x.shape[0]
    tm = ROW_TILE
    steps = n // tm
    final = layer == DEPTH - 1
    row = lambda w: pl.BlockSpec((tm, w), lambda i: (i, 0))
    whole = lambda a: pl.BlockSpec(a.shape, lambda i: (0, 0))
    in_specs = [row(D_MODEL), row(W_R), row(W_A),
                _mod_spec(layer, rows_per_seq, tm, ctx),
                _layer_spec(g2, layer), whole(wout), whole(w1), whole(w2)]
    args = [x, o_r, o_a, mod, g2, wout, w1, w2]
    out_shape = [jax.ShapeDtypeStruct((n, D_MODEL), F32)]
    out_specs = [row(D_MODEL)]
    if final:
        in_specs.append(pl.BlockSpec(gf.shape, lambda i: (0, 0)))
        args.append(gf)
    else:
        in_specs += [_layer_spec(g1, layer + 1), _mod_spec(layer + 1, rows_per_seq, tm, ctx)]
        args += [g1, mod]
        out_shape.append(jax.ShapeDtypeStruct((n, D_MODEL), BF16))
        out_specs.append(row(D_MODEL))
    for a in cast:
        r, c = a.shape[1:]
        in_specs.append(pl.BlockSpec((None, r // steps, c), lambda i: (layer + 1, i, 0)))
        args.append(a)
        out_shape.append(jax.ShapeDtypeStruct((r, c), BF16))
        out_specs.append(pl.BlockSpec((r // steps, c), lambda i: (i, 0)))
    res = pl.pallas_call(
        functools.partial(_outffn_kernel, final=final, n_cast=len(cast)),
        out_shape=out_shape,
        grid=(steps,),
        in_specs=in_specs,
        out_specs=out_specs,
        scratch_shapes=[pltpu.VMEM((tm, D_FF), BF16)],
        compiler_params=_params("arbitrary"),
        name="outffn",
    )(*args)
    if final:
        return (res[0], None)
    return tuple(res)


def _rope_tables(seq):
    rows = seq // GRID_W
    rowi = jnp.repeat(jnp.arange(rows, dtype=F32), GRID_W)
    coli = jnp.tile(jnp.arange(GRID_W, dtype=F32), rows)
    inv = ROPE_THETA ** (-jnp.arange(0, ROPE_AXIS_DIM, 2, dtype=F32) / ROPE_AXIS_DIM)
    ar = rowi[:, None] * inv[None, :]
    ac = coli[:, None] * inv[None, :]
    ang = jnp.concatenate([ar, ar, ac, ac], axis=-1)
    cos = jnp.tile(jnp.cos(ang), (1, LANES // HEAD_DIM))
    sin = jnp.tile(jnp.sin(ang), (1, LANES // HEAD_DIM))
    first = (jnp.arange(LANES) % ROPE_AXIS_DIM) < ROPE_AXIS_DIM // 2
    return cos, jnp.where(first, -sin, 0.0), jnp.where(first, 0.0, sin)


def kernel(x_prompt, x_sample, cache_k, cache_v, state_hgrn, c, c_ctx, w_mod, b_mod, norm1_g, w_in, lb_raw,
           hgrn_norm_g, q_norm_g, k_norm_g, w_out, norm2_g, w1, w2, final_norm_g):
    bp, sp, _ = x_prompt.shape
    bs, ss, _ = x_sample.shape
    past = cache_k.shape[2]

    cvec = jnp.zeros((MOD_ROWS, D_MODEL), F32).at[0:bs].set(c).at[MOD_ROWS // 2].set(c_ctx)
    mod = _modulation(cvec, w_mod, b_mod).reshape(DEPTH, MOD_ROWS, 1, MOD_W)

    stacked = (w_in, w_out, w1, w2)
    w_in_b, w_out_b, w1_b, w2_b = (w[0].astype(BF16) for w in stacked)
    ck =cache_k.reshape(bs, DEPTH, past, KV_A * HEAD_DIM)
    cv = cache_v.reshape(bs, DEPTH, past, KV_A * HEAD_DIM)
    tabs = _rope_tables(ss)
    gf = final_norm_g.reshape(1, D_MODEL)
    g1 = norm1_g.reshape(DEPTH, 1, D_MODEL)
    g2 = norm2_g.reshape(DEPTH, 1, D_MODEL)
    gn = jnp.tile(hgrn_norm_g, (1, H_R)).reshape(DEPTH, 1, W_R)
    qg = jnp.tile(q_norm_g, (1, H_A)).reshape(DEPTH, 1, W_A)
    kg = jnp.tile(k_norm_g, (1, KV_A)).reshape(DEPTH, 1, LANES)

    xp = x_prompt.reshape(bp * sp, D_MODEL)
    xs = x_sample.reshape(bs * ss, D_MODEL)
    hp = _first_prenorm(xp, mod, g1, sp, True)
    hs = _first_prenorm(xs, mod, g1, ss, False)
    new_s, new_kv = None, None
    for l in range(DEPTH):
        o_r, new_s = _hgrn(hp, w_in_b, lb_raw, gn, l, bp, sp, s_prev=new_s, want_s=True)
        o_a, *new_kv = _attn(hp, w_in_b, qg, kg, bp, sp, l, kv_prev=new_kv)
        xp, hp, *next_w = _outffn(xp, o_r, o_a, mod, g1, g2, w_out_b, w1_b, w2_b, gf, l, sp, True,
                                  cast=stacked if l + 1 < DEPTH else ())

        o_r, _ = _hgrn(hs, w_in_b, lb_raw, gn, l, bs, ss, s0=state_hgrn)
        o_a, = _attn(hs, w_in_b, qg, kg, bs, ss, l, cache=(ck, cv), rope_tabs=tabs)
        xs, hs = _outffn(xs, o_r, o_a, mod, g1, g2, w_out_b, w1_b, w2_b, gf, l, ss, False)
        if next_w:
            w_in_b, w_out_b, w1_b, w2_b = next_w

    y_prompt = xp.reshape(bp, sp, D_MODEL)
    y_sample = xs.reshape(bs, ss, D_MODEL)
    new_k = new_kv[0].reshape(bp, DEPTH, sp, KV_A, HEAD_DIM)
    new_v = new_kv[1].reshape(bp, DEPTH, sp, KV_A, HEAD_DIM)
    return (y_prompt, y_sample, new_k, new_v, new_s)
```

```python
import functools

import jax
import jax.numpy as jnp
import numpy as np
from jax import lax
from jax.experimental import pallas as pl
from jax.experimental.pallas import tpu as pltpu

F32 = jnp.float32
BF16 = jnp.bfloat16

D_MODEL = 1024
DEPTH = 4
GRID_W = 64
D_FF = 4 * D_MODEL
H_R = 4
DK_R = 128
DV_R = 128
W_R = H_R * DV_R
H_A = 8
KV_A = 2
HEAD_DIM = 64
W_A = H_A * HEAD_DIM
ROPE_AXIS_DIM = HEAD_DIM // 2
ROPE_THETA = 10000.0
EPS = 1e-6
LOG2E = 1.4426950408889634
IN_R = 3 * H_R * DK_R + 2 * W_R
IN_A = W_A + 2 * KV_A * HEAD_DIM
MOD_W = 6 * D_MODEL
MOD_ROWS = 8

LANES = 128
SUBLANES = 8
VMEM_LIMIT = 56 * 1024 * 1024

HGRN_CHUNK = 128
FAST_CHUNK = 64
FAST_UNROLL = 8
FAST_LIMIT = 115.0
PROJ_ROWS = 256
HGRN_ROWS = 512
ATTN_QROWS = 256
ATTN_KEYS = 256
ATTN_ROWS = 1024
BOUND_SLACK = 1.03
BOUND_LIMIT = 40.0
ATTN_LOOKAHEAD = 2
FFN_CHUNK = 1024
FFN_PARTS = 2
ROW_TILE = 512


def _dot(a, b):
    return jnp.dot(a, b, preferred_element_type=F32)


def _dot_nt(a, b):
    return lax.dot_general(a, b, (((1,), (1,)), ((), ())), preferred_element_type=F32)


def _in_turn(gens):
    active = list(gens)
    while active:
        for gen in list(active):
            if next(gen, "done") == "done":
                active.remove(gen)


def _params(*sem):
    return pltpu.CompilerParams(dimension_semantics=sem, vmem_limit_bytes=VMEM_LIMIT)


def _rms(x):
    return x * lax.rsqrt(jnp.mean(x * x, axis=-1, keepdims=True) + EPS)


def _mod_spec(layer, rows_per_seq, tile, ctx):
    per = rows_per_seq // tile
    row = (lambda i: MOD_ROWS // 2) if ctx else (lambda i: i // per)
    return pl.BlockSpec((None, None, 1, MOD_W), lambda i: (layer, row(i), 0, 0))


def _layer_spec(a, layer):
    return pl.BlockSpec((None,) + a.shape[1:], lambda i: (layer, 0, 0))


def _prenorm(x, shift, scale, g):
    return (_rms(x) * g * (1.0 + scale) + shift).astype(BF16)


def _mod_kernel(c_ref, w_ref, b_ref, o_ref):
    c = c_ref[...]
    s = c / (1.0 + jnp.exp(-c))
    o_ref[...] = _dot(s.astype(BF16), w_ref[...].astype(BF16)) + b_ref[...]


def _modulation(cvec, w_mod, b_mod):
    tn = 1536
    return pl.pallas_call(
        _mod_kernel,
        out_shape=jax.ShapeDtypeStruct((DEPTH, MOD_ROWS, MOD_W), F32),
        grid=(DEPTH, MOD_W // tn),
        in_specs=[
            pl.BlockSpec((MOD_ROWS, D_MODEL), lambda l, j: (0, 0)),
            pl.BlockSpec((None, D_MODEL, tn), lambda l, j: (l, 0, j)),
            pl.BlockSpec((None, 1, tn), lambda l, j: (l, 0, j)),
        ],
        out_specs=pl.BlockSpec((None, MOD_ROWS, tn), lambda l, j: (l, 0, j)),
        compiler_params=_params("arbitrary", "arbitrary"),
        name="modulation",
    )(cvec, w_mod, b_mod.reshape(DEPTH, 1, MOD_W))


def _prenorm_kernel(x_ref, mod_ref, g_ref, o_ref):
    o_ref[...] = _prenorm(x_ref[...], mod_ref[:, 0:D_MODEL], mod_ref[:, D_MODEL:2 * D_MODEL], g_ref[...])


def _first_prenorm(x, mod, g1, rows_per_seq, ctx):
    n = x.shape[0]
    tm = ROW_TILE
    return pl.pallas_call(
        _prenorm_kernel,
        out_shape=jax.ShapeDtypeStruct((n, D_MODEL), BF16),
        grid=(n // tm,),
        in_specs=[pl.BlockSpec((tm, D_MODEL), lambda i: (i, 0)),
                  _mod_spec(0, rows_per_seq, tm, ctx),
                  _layer_spec(g1, 0)],
        out_specs=pl.BlockSpec((tm, D_MODEL), lambda i: (i, 0)),
        compiler_params=_params("arbitrary"),
        name="prenorm",
    )(x, mod, g1)


def _bcast_block_row(b, blk, r):
    n, w = b.shape
    parts = [jnp.broadcast_to(b[k * blk + r:k * blk + r + 1, :], (blk, w)) for k in range(n // blk)]
    return parts[0] if len(parts) == 1 else jnp.concatenate(parts, axis=0)


def _hgrn_kernel(*refs, layer, seq, nseq, has_s0, has_prev, want_s):
    L = HGRN_CHUNK
    n_chunks = seq // L
    n_levels = int(np.log2(L))
    refs = list(refs)
    h_ref, w_ref, lbraw_ref, gn_ref = refs[:4]
    pos = 4
    s0_ref = None
    if has_s0:
        s0_ref = refs[pos]
        pos += 1
    if has_prev:
        pos += 1
    o_ref = refs[pos]
    pos += 1
    s_ref = None
    if want_s:
        s_ref = refs[pos]
        pos += 1
    proj_ref, of_ref, ob_ref, st_ref, code_ref, kg_ref, b_ref = refs[pos:pos + 7]
    q_cols, i_cols, g_cols = (slice(n * W_R, (n + 1) * W_R) for n in (0, 3, 4))
    f_cols = [slice(W_R, 2 * W_R), slice(2 * W_R, 3 * W_R)]

    def project(g):
        rows = slice(g * PROJ_ROWS, (g + 1) * PROJ_ROWS)
        proj_ref[rows, :] = _dot(h_ref[rows, :], w_ref[...])

    if layer > 0:
        raw = lbraw_ref[...]
        e = jnp.exp(raw - jnp.max(raw, axis=0, keepdims=True))
        p = e / jnp.sum(e, axis=0, keepdims=True)
        lb = p[1]
        for j in range(2, layer + 1):
            lb = lb + p[j]
        log_lb = jnp.log(lb)
        log_1m_lb = jnp.log1p(-lb)
        one_m_lb = 1.0 - lb

    def gates(fx, d):
        ls = jnp.minimum(fx, 0.0) - jnp.log(1.0 + jnp.exp(-jnp.abs(fx)))
        k = jnp.exp(ls - fx)
        if layer == 0:
            return ls, k
        a = log_lb[d:d + 1, :]
        c = log_1m_lb[d:d + 1, :] + ls
        lf = jnp.maximum(a, c) + jnp.log(1.0 + jnp.exp(-jnp.abs(a - c)))
        return lf, one_m_lb[d:d + 1, :] * k

    ti = lax.broadcasted_iota(jnp.int32, (L, L), 0)
    si = lax.broadcasted_iota(jnp.int32, (L, L), 1)
    x = ti ^ si
    lv = jnp.zeros((L, L), jnp.int32)
    for j in range(n_levels):
        lv = lv + jnp.where((x >> j) != 0, 1, 0)
    code_ref[0] = jnp.where(ti >= si, lv, -1)
    code_ref[1] = jnp.where(ti <= si, lv, -1)

    def init_states():
        for s in range(nseq):
            for d in range(2):
                for h in range(H_R):
                    if has_s0:
                        st_ref[s, d, h] = s0_ref[s, d, h].T
                    else:
                        st_ref[s, d, h] = jnp.zeros((DV_R, DK_R), F32)

    init_states()

    def cumulative(lf, tri):
        hi = lf.astype(BF16)
        r1 = lf - hi.astype(F32)
        mid = r1.astype(BF16)
        lo = (r1 - mid.astype(F32)).astype(BF16)
        b3 = _dot(tri, jnp.concatenate([hi, mid, lo], axis=1))
        return b3[:, 0:W_R] + b3[:, W_R:2 * W_R] + b3[:, 2 * W_R:3 * W_R]

    Lf = FAST_CHUNK
    n_fast = seq // Lf
    unroll = min(FAST_UNROLL, n_fast)
    ref_row = [Lf // 2 - 1, Lf // 2]
    first_row = [0, Lf - 1]
    last_row = [Lf - 1, 0]
    tf = lax.broadcasted_iota(jnp.int32, (Lf, Lf), 0)
    sf = lax.broadcasted_iota(jnp.int32, (Lf, Lf), 1)
    keep = [tf >= sf, tf <= sf]
    tf2 = lax.broadcasted_iota(jnp.int32, (Lf, 2 * Lf), 0)
    sf2 = lax.broadcasted_iota(jnp.int32, (Lf, 2 * Lf), 1) & (Lf - 1)
    keep2 = [tf2 >= sf2, tf2 <= sf2]

    def prepare(g, worst):
        for i in range(PROJ_ROWS // Lf):
            rows = slice(g * PROJ_ROWS + i * Lf, g * PROJ_ROWS + (i + 1) * Lf)
            q_top = jnp.max(jnp.abs(proj_ref[rows, q_cols]), axis=0, keepdims=True)
            q_bits = jnp.log(jnp.maximum(q_top, 1.0)) * LOG2E
            for d in range(2):
                lf, k = gates(proj_ref[rows, f_cols[d]], d)
                b = cumulative(lf * LOG2E, jnp.where(keep[d], 1.0, 0.0).astype(BF16))
                kg_ref[d, rows, :] = k
                b_ref[d, rows, :] = b
                b_mid = b[ref_row[d]:ref_row[d] + 1, :]
                grow = jnp.maximum(b[first_row[d]:first_row[d] + 1, :] - b_mid,
                                   b_mid - b[last_row[d]:last_row[d] + 1, :])
                worst = jnp.maximum(worst, grow + q_bits)
        return worst

    def fast_direction(d, s, c, out_ref):
        start = s * seq + c * Lf
        rows = slice(start, start + Lf) if isinstance(start, int) else pl.ds(pl.multiple_of(start, Lf), Lf)
        q = proj_ref[rows, q_cols]
        v = proj_ref[rows, i_cols]
        k = kg_ref[d, rows, :]
        b = b_ref[d, rows, :]
        b_mid = b[ref_row[d]:ref_row[d] + 1, :]
        edge = b[last_row[d]:last_row[d] + 1, :]
        q_t = (q * jnp.exp2(b - b_mid)).astype(BF16)
        k_f = k * jnp.exp2(b_mid - b)
        k_t = k_f.astype(BF16)
        k_out = (k_f * jnp.exp2(edge - b_mid)).astype(BF16)
        dec = jnp.exp2(edge)
        from_start = jnp.exp2(b_mid)
        vb = v.astype(BF16)
        zero_s = jnp.zeros((DV_R, DK_R), BF16)
        zero_r = jnp.zeros((Lf, DK_R), BF16)

        def side_by_side(x1, x2, zero):
            return jnp.concatenate([jnp.concatenate([x1, zero], axis=1),
                                    jnp.concatenate([zero, x2], axis=1)], axis=0)

        both = []
        for h in range(0, H_R, 2):
            c1 = slice(h * DK_R, (h + 1) * DK_R)
            c2 = slice((h + 1) * DK_R, (h + 2) * DK_R)
            cp = slice(h * DK_R, (h + 2) * DK_R)
            st1 = st_ref[s, d, h]
            st2 = st_ref[s, d, h + 1]
            rhs = jnp.concatenate([
                side_by_side((st1 * from_start[:, c1]).astype(BF16), (st2 * from_start[:, c2]).astype(BF16), zero_s),
                side_by_side(k_t[:, c1], k_t[:, c2], zero_r)], axis=0)
            both.append(_dot_nt(q_t[:, cp], rhs))
            v_rows = jnp.concatenate([v[:, c1], v[:, c2]], axis=0)
            upd = _dot(v_rows.T.astype(BF16), side_by_side(k_out[:, c1], k_out[:, c2], zero_r))
            st_ref[s, d, h] = st1 * dec[:, c1] + upd[:, 0:DK_R]
            st_ref[s, d, h + 1] = st2 * dec[:, c2] + upd[:, DK_R:2 * DK_R]
        yield
        outs = []
        for n, h in enumerate(range(0, H_R, 2)):
            c1 = slice(h * DK_R, (h + 1) * DK_R)
            c2 = slice((h + 1) * DK_R, (h + 2) * DK_R)
            a = jnp.where(keep2[d], both[n][:, 2 * DV_R:2 * DV_R + 2 * Lf], 0.0).astype(BF16)
            outs.append(both[n][:, 0:2 * DV_R] + _dot(a, side_by_side(vb[:, c1], vb[:, c2], zero_r)))
        out_ref[rows, :] = jnp.concatenate(outs, axis=1)

    def fast_chunks(s, first):
        gens = []
        for i in range(unroll):
            gens.append(fast_direction(0, s, first + i, of_ref))
            gens.append(fast_direction(1, s, n_fast - 1 - (first + i), ob_ref))
        _in_turn(gens)

    n_groups = nseq * seq // PROJ_ROWS
    inline = n_fast == unroll
    worst = jnp.zeros((1, W_R), F32)
    project(0)
    for g in range(n_groups):
        if g + 1 < n_groups:
            project(g + 1)
        worst = prepare(g, worst)
        if inline and ((g + 1) * PROJ_ROWS) % seq == 0:
            fast_chunks((g + 1) * PROJ_ROWS // seq - 1, 0)
    if not inline:
        for s in range(nseq):
            def fast_step(c, carry, s=s):
                fast_chunks(s, c * unroll)
                return carry
            lax.fori_loop(0, n_fast // unroll, fast_step, 0)

    def direction(d, s, c, out_ref):
        rows = pl.ds(pl.multiple_of(s * seq + c * L, L), L)
        row = lax.broadcasted_iota(jnp.int32, (L, W_R), 0)
        code = code_ref[d]
        tri = jnp.where(code >= 0, 1.0, 0.0).astype(BF16)
        q = proj_ref[rows, q_cols]
        v = proj_ref[rows, i_cols]
        lf, k = gates(proj_ref[rows, f_cols[d]], d)
        lf = lf * LOG2E
        b = cumulative(lf, tri)
        yield

        vb = v.astype(BF16)
        a_h = [None] * H_R
        for j in range(1, n_levels + 1):
            half = 1 << (j - 1)
            blk = 2 * half
            u = row & (blk - 1)
            qside = (u >= half) if d == 0 else (u < half)
            if j == 1:
                e_j = jnp.where(qside, lf, 0.0)
            elif j == 2:
                up = pltpu.roll(lf, L - 1, axis=0)
                dn = pltpu.roll(lf, 1, axis=0)
                if d == 0:
                    e_j = jnp.where(u == 0, up, jnp.where(u == 1, 0.0, jnp.where(u == 2, lf, lf + dn)))
                else:
                    e_j = jnp.where(u == 0, lf + up, jnp.where(u == 1, lf, jnp.where(u == 2, 0.0, dn)))
            else:
                diff = b - _bcast_block_row(b, blk, half - 1 + d)
                e_j = jnp.where(qside, diff, -diff)
            z = (jnp.where(qside, q, k) * jnp.exp2(e_j)).astype(BF16)
            for h in range(H_R):
                cs = slice(h * DK_R, (h + 1) * DK_R)
                p_j = _dot_nt(z[:, cs], z[:, cs])
                a_h[h] = jnp.where(code == j, p_j, 0.0 if a_h[h] is None else a_h[h])
            yield

        edge = b[L - 1:L, :] if d == 0 else b[0:1, :]
        q_in = (q * jnp.exp2(b)).astype(BF16)
        k_out = (k * jnp.exp2(edge - b)).astype(BF16)
        dec = jnp.exp2(edge)
        qk = q * k
        outs = []
        for h in range(H_R):
            cs = slice(h * DK_R, (h + 1) * DK_R)
            st = st_ref[s, d, h]
            o = _dot(a_h[h].astype(BF16), vb[:, cs]) + _dot_nt(q_in[:, cs], st.astype(BF16))
            o = o + jnp.sum(qk[:, cs], axis=-1, keepdims=True) * v[:, cs]
            st_ref[s, d, h] = st * dec[:, cs] + _dot(v[:, cs].T.astype(BF16), k_out[:, cs])
            outs.append(o)
        out_ref[rows, :] = jnp.concatenate(outs, axis=1)

    def step(i, carry):
        s, c = i // n_chunks, i % n_chunks
        _in_turn([direction(0, s, c, of_ref), direction(1, s, n_chunks - 1 - c, ob_ref)])
        return carry

    @pl.when(jnp.max(worst) > FAST_LIMIT)
    def _():
        init_states()
        lax.fori_loop(0, nseq * n_chunks, step, 0)

    def finish(c, carry):
        rows = pl.ds(pl.multiple_of(c * L, L), L)
        o = of_ref[rows, :] + ob_ref[rows, :]
        g = proj_ref[rows, g_cols]
        parts = [_rms(o[:, h * DV_R:(h + 1) * DV_R]) for h in range(H_R)]
        y = jnp.concatenate(parts, axis=1) * gn_ref[...]
        o_ref[rows, :] = y * (g / (1.0 + jnp.exp(-g)))
        return carry

    lax.fori_loop(0, nseq * n_chunks, finish, 0)

    if want_s:
        for s in range(nseq):
            own = s_ref.at[s] if has_prev else s_ref.at[s, layer]
            for d in range(2):
                for h in range(H_R):
                    own[d, h] = st_ref[s, d, h].T
            if not has_prev:
                for other in range(DEPTH):
                    if other != layer:
                        s_ref[s, other] = jnp.zeros((2, H_R, DK_R, DV_R), F32)


def _hgrn(hb, w_in, lb_raw, gn, layer, batch, seq, s0=None, s_prev=None, want_s=False):
    has_s0 = s0 is not None
    nseq = max(1, HGRN_ROWS // seq)
    rows = nseq * seq
    state_spec = pl.BlockSpec((nseq, None, 2, H_R, DK_R, DV_R), lambda b: (b, layer, 0, 0, 0, 0))
    in_specs = [pl.BlockSpec((rows, D_MODEL), lambda b: (b, 0)),
                pl.BlockSpec((D_MODEL, IN_R), lambda b: (0, 0)),
                pl.BlockSpec((DEPTH, 2, W_R), lambda b: (0, 0, 0)),
                _layer_spec(gn, layer)]
    args = [hb, w_in, lb_raw, gn]
    if has_s0:
        in_specs.append(state_spec)
        args.append(s0)
    out_shape = [jax.ShapeDtypeStruct((batch * seq, W_R), F32)]
    out_specs = [pl.BlockSpec((rows, W_R), lambda b: (b, 0))]
    aliases = {}
    if want_s:
        out_shape.append(jax.ShapeDtypeStruct((batch, DEPTH, 2, H_R, DK_R, DV_R), F32))
        if s_prev is not None:
            out_specs.append(state_spec)
            aliases[len(args)] = 1
            in_specs.append(pl.BlockSpec(memory_space=pl.ANY))
            args.append(s_prev)
        else:
            out_specs.append(pl.BlockSpec((nseq, DEPTH, 2, H_R, DK_R, DV_R), lambda b: (b, 0, 0, 0, 0, 0)))
    res = pl.pallas_call(
        functools.partial(_hgrn_kernel, layer=layer, seq=seq, nseq=nseq, has_s0=has_s0,
                          has_prev=bool(aliases), want_s=want_s),
        out_shape=out_shape,
        grid=(batch // nseq,),
        in_specs=in_specs,
        out_specs=out_specs,
        scratch_shapes=[pltpu.VMEM((rows, IN_R), F32),
                        pltpu.VMEM((rows, W_R), F32), pltpu.VMEM((rows, W_R), F32),
                        pltpu.VMEM((nseq, 2, H_R, DV_R, DK_R), F32),
                        pltpu.VMEM((2, HGRN_CHUNK, HGRN_CHUNK), jnp.int32),
                        pltpu.VMEM((2, rows, W_R), F32), pltpu.VMEM((2, rows, W_R), F32)],
        input_output_aliases=aliases,
        compiler_params=_params("arbitrary"),
        name="hgrn",
    )(*args)
    return res if want_s else (res[0], None)


def _head_mean_square(x):
    w = x.shape[1]
    shift = HEAD_DIM.bit_length() - 1
    r = lax.broadcasted_iota(jnp.int32, (w, w), 0) >> shift
    c = lax.broadcasted_iota(jnp.int32, (w, w), 1) >> shift
    ones = jnp.where(r == c, 1.0, 0.0).astype(BF16)
    return _dot((x * x).astype(BF16), ones) * (1.0 / HEAD_DIM)


def _attn_kernel(*refs, seq, nseq, past, n_prev, layer):
    refs = list(refs)
    h_ref, wq0_ref, wq1_ref, wkv_ref, qg_ref, kg_ref = refs[:6]
    pos = 6
    if past:
        ck_ref, cv_ref, cos_ref, sa_ref, sb_ref = refs[pos:pos + 5]
        pos += 5
        o_ref = refs[pos]
        pos += 1
    else:
        pos += n_prev
        o_ref, kn_ref, vn_ref = refs[pos:pos + 3]
        pos += 3
    qs_ref, ks_ref, vt_ref = refs[pos:pos + 3]

    def rope(x):
        up = pltpu.roll(x, LANES - ROPE_AXIS_DIM // 2, axis=1)
        dn = pltpu.roll(x, ROPE_AXIS_DIM // 2, axis=1)
        return x * cos_ref[...] + up * sa_ref[...] + dn * sb_ref[...]

    hb = h_ref[...]
    q = jnp.concatenate([_dot(hb, wq0_ref[...]), _dot(hb, wq1_ref[...])], axis=1)
    kv = _dot(hb, wkv_ref[...])
    k = kv[:, 0:LANES]
    v = kv[:, LANES:2 * LANES]
    qn = q * lax.rsqrt(_head_mean_square(q) + EPS) * qg_ref[...]
    kn = k * lax.rsqrt(_head_mean_square(k) + EPS) * kg_ref[...]
    scale = HEAD_DIM ** -0.5 * LOG2E
    low = lax.broadcasted_iota(jnp.int32, (nseq * seq, LANES), 1) < HEAD_DIM
    pairs_per_kv = (W_A // LANES) // KV_A
    q_top2 = HEAD_DIM * scale * scale * jnp.max(qg_ref[...] * qg_ref[...])
    k_new_top2 = HEAD_DIM * jnp.max(kg_ref[...] * kg_ref[...])
    for p in range(W_A // LANES):
        cs = slice(p * LANES, (p + 1) * LANES)
        qp = (rope(qn[:, cs]) if past else qn[:, cs]) * scale
        even = jnp.where(low, qp, 0.0)
        odd = jnp.where(low, 0.0, qp)
        if p // pairs_per_kv == 0:
            odd = pltpu.roll(odd, HEAD_DIM, axis=1)
        else:
            even = pltpu.roll(even, HEAD_DIM, axis=1)
        qs_ref[0, :, cs] = even.astype(BF16)
        qs_ref[1, :, cs] = odd.astype(BF16)
    k_top2 = k_new_top2
    if past:
        k_top2 = jnp.maximum(k_top2, jnp.max(_head_mean_square(ck_ref[...])) * HEAD_DIM)
        ks_ref[0] = jnp.concatenate([ck_ref[...], rope(kn)], axis=0).astype(BF16)
        vt_ref[0] = jnp.concatenate([cv_ref[...], v], axis=0).T.astype(BF16)
    else:
        for i in range(nseq):
            rs = slice(i * seq, (i + 1) * seq)
            ks_ref[i] = kn[rs, :].astype(BF16)
            vt_ref[i] = v[rs, :].T.astype(BF16)
            if n_prev:
                kn_ref[i] = kn[rs, :]
                vn_ref[i] = v[rs, :]
            else:
                for other in range(DEPTH):
                    kn_ref[i, other] = kn[rs, :] if other == layer else jnp.zeros((seq, LANES), F32)
                    vn_ref[i, other] = v[rs, :] if other == layer else jnp.zeros((seq, LANES), F32)

    tq = min(ATTN_QROWS, seq)
    total = past + seq

    kb = ATTN_KEYS if total % ATTN_KEYS == 0 else total
    n_blocks = total // kb
    n_tiles = seq // tq
    seqs = range(nseq) if n_tiles == 1 else range(1)
    steps = [(i, p, kc) for i in seqs for p in range(W_A // LANES) for kc in range(n_blocks)]

    worst = jnp.sqrt(q_top2 * k_top2) * BOUND_SLACK

    def tile(r, carry, bounded):
        def q_rows(i):
            start = i * seq + r * tq
            return slice(start, start + tq) if isinstance(start, int) else pl.ds(pl.multiple_of(start, tq), tq)

        def q_pair(i, p):
            cs = slice(p * LANES, (p + 1) * LANES)
            return jnp.concatenate([qs_ref[0, q_rows(i), cs], qs_ref[1, q_rows(i), cs]], axis=0)

        def scores(step):
            i, p, kc = step
            return _dot_nt(ks_ref[i, kc * kb:(kc + 1) * kb, :], q_pair(i, p))

        ahead = [scores(st) for st in steps[:ATTN_LOOKAHEAD]]
        for n, (i, p, kc) in enumerate(steps):
            s = ahead.pop(0)
            if n + ATTN_LOOKAHEAD < len(steps):
                ahead.append(scores(steps[n + ATTN_LOOKAHEAD]))
            kh = p // pairs_per_kv
            v_t = vt_ref[i, kh * HEAD_DIM:(kh + 1) * HEAD_DIM, kc * kb:(kc + 1) * kb]
            if bounded:
                pexp = jnp.exp2(s - worst)
                part = jnp.sum(pexp, axis=0, keepdims=True)
                prod = _dot(v_t, pexp.astype(BF16))
                den = part if kc == 0 else den + part
                o2 = prod if kc == 0 else o2 + prod
            else:
                s_max = jnp.max(s, axis=0, keepdims=True)
                if kc == 0:
                    m = s_max
                    pexp = jnp.exp2(s - m)
                    den = jnp.sum(pexp, axis=0, keepdims=True)
                    o2 = _dot(v_t, pexp.astype(BF16))
                else:
                    m_new = jnp.maximum(m, s_max)
                    alpha = jnp.exp2(m - m_new)
                    pexp = jnp.exp2(s - m_new)
                    den = den * alpha + jnp.sum(pexp, axis=0, keepdims=True)
                    o2 = o2 * alpha + _dot(v_t, pexp.astype(BF16))
                    m = m_new
            if kc == n_blocks - 1:
                o2 = o2 / den
                o_ref[q_rows(i), p * LANES:(p + 1) * LANES] = jnp.concatenate([o2[:, 0:tq], o2[:, tq:2 * tq]], axis=0).T
        return carry

    def tiles(bounded):
        if n_tiles == 1:
            tile(0, 0, bounded)
        else:
            assert nseq == 1
            lax.fori_loop(0, n_tiles, functools.partial(tile, bounded=bounded), 0)

    lax.cond(worst <= BOUND_LIMIT, lambda: tiles(True), lambda: tiles(False))


def _attn(hb, w_in, qg, kg, batch, seq, layer, cache=None, rope_tabs=None, kv_prev=None):
    past = 0 if cache is None else cache[0].shape[2]
    wide = 2 * LANES
    w_specs = [pl.BlockSpec((D_MODEL, wide), lambda b, j=j: (0, IN_R // wide + j))
               for j in range(IN_A // wide)]
    nseq = max(1, ATTN_ROWS // seq)
    rows = nseq * seq
    in_specs = [pl.BlockSpec((rows, D_MODEL), lambda b: (b, 0)), *w_specs,
                _layer_spec(qg, layer), _layer_spec(kg, layer)]
    args = [hb, w_in, w_in, w_in, qg, kg]
    out_shape = [jax.ShapeDtypeStruct((batch * seq, W_A), F32)]
    out_specs = [pl.BlockSpec((rows, W_A), lambda b: (b, 0))]
    aliases = {}
    if past:
        ck, cv = cache
        cspec = pl.BlockSpec((None, None, past, LANES), lambda b: (b, layer, 0, 0))
        tspec = pl.BlockSpec((seq, LANES), lambda b: (0, 0))
        in_specs += [cspec, cspec, tspec, tspec, tspec]
        args += [ck, cv, *rope_tabs]
    else:
        for i in range(2):
            out_shape.append(jax.ShapeDtypeStruct((batch, DEPTH, seq, LANES), F32))
            if kv_prev is not None:
                out_specs.append(pl.BlockSpec((nseq, None, seq, LANES), lambda b: (b, layer, 0, 0)))
                aliases[len(args)] = 1 + i
                in_specs.append(pl.BlockSpec(memory_space=pl.ANY))
                args.append(kv_prev[i])
            else:
                out_specs.append(pl.BlockSpec((nseq, DEPTH, seq, LANES), lambda b: (b, 0, 0, 0)))
    res = pl.pallas_call(
        functools.partial(_attn_kernel, seq=seq, nseq=nseq, past=past, n_prev=len(aliases), layer=layer),
        out_shape=out_shape,
        grid=(batch // nseq,),
        in_specs=in_specs,
        out_specs=out_specs,
        scratch_shapes=[pltpu.VMEM((2, rows, W_A), BF16),
                        pltpu.VMEM((nseq, past + seq, LANES), BF16),
                        pltpu.VMEM((nseq, LANES, past + seq), BF16)],
        input_output_aliases=aliases,
        compiler_params=_params("arbitrary"),
        name="attn",
    )(*args)
    return res


def _outffn_kernel(*refs, final, n_cast):
    refs = list(refs)
    x_ref, or_ref, oa_ref, mod_ref, g2_ref, wout_ref, w1_ref, w2_ref, last_ref = refs[:9]
    pos = 9
    if not final:
        nmod_ref = refs[pos]
        pos += 1
    n_out = 1 if final else 2
    for src, dst in zip(refs[pos:pos + n_cast], refs[pos + n_cast + n_out:pos + 2 * n_cast + n_out]):
        dst[...] = src[...].astype(BF16)
    pos += n_cast
    o_ref = refs[pos]
    gate1 = mod_ref[:, 2 * D_MODEL:3 * D_MODEL]
    shift2 = mod_ref[:, 3 * D_MODEL:4 * D_MODEL]
    scale2 = mod_ref[:, 4 * D_MODEL:5 * D_MODEL]
    gate2 = mod_ref[:, 5 * D_MODEL:6 * D_MODEL]
    hid_ref = refs[-1]

    def rows_part(rs):
        mix = _dot(or_ref[rs, :].astype(BF16), wout_ref[0:W_R, :]) + _dot(oa_ref[rs, :].astype(BF16), wout_ref[W_R:W_R + W_A, :])
        yield
        x1 = x_ref[rs, :] + gate1 * mix
        h2 = _prenorm(x1, shift2, scale2, g2_ref[...])
        for j in range(D_FF // FFN_CHUNK):
            cs = slice(j * FFN_CHUNK, (j + 1) * FFN_CHUNK)
            hid = jnp.maximum(_dot(h2, w1_ref[:, cs]), 0.0)
            hid_ref[rs, cs] = (hid * hid).astype(BF16)
        yield
        x2 = x1 + gate2 * _dot(hid_ref[rs, :], w2_ref[...])
        yield
        if final:
            o_ref[rs, :] = _rms(x2) * last_ref[...]
        else:
            o_ref[rs, :] = x2
            hn_ref = refs[pos + 1]
            hn_ref[rs, :] = _prenorm(x2, nmod_ref[:, 0:D_MODEL], nmod_ref[:, D_MODEL:2 * D_MODEL], last_ref[...])

    n_rows = x_ref.shape[0]
    part = n_rows // FFN_PARTS
    _in_turn([rows_part(slice(i * part, (i + 1) * part)) for i in range(FFN_PARTS)])


def _outffn(x, o_r, o_a, mod, g1, g2, wout, w1, w2, gf, layer, rows_per_seq, ctx, cast=()):
    n = x.shape[0]
    tm = ROW_TILE
    steps = n // tm
    final = layer == DEPTH - 1
    row = lambda w: pl.BlockSpec((tm, w), lambda i: (i, 0))
    whole = lambda a: pl.BlockSpec(a.shape, lambda i: (0, 0))
    in_specs = [row(D_MODEL), row(W_R), row(W_A),
                _mod_spec(layer, rows_per_seq, tm, ctx),
                _layer_spec(g2, layer), whole(wout), whole(w1), whole(w2)]
    args = [x, o_r, o_a, mod, g2, wout, w1, w2]
    out_shape = [jax.ShapeDtypeStruct((n, D_MODEL), F32)]
    out_specs = [row(D_MODEL)]
    if final:
        in_specs.append(pl.BlockSpec(gf.shape, lambda i: (0, 0)))
        args.append(gf)
    else:
        in_specs += [_layer_spec(g1, layer + 1), _mod_spec(layer + 1, rows_per_seq, tm, ctx)]
        args += [g1, mod]
        out_shape.append(jax.ShapeDtypeStruct((n, D_MODEL), BF16))
        out_specs.append(row(D_MODEL))
    for a in cast:
        r, c = a.shape[1:]
        in_specs.append(pl.BlockSpec((None, r // steps, c), lambda i: (layer + 1, i, 0)))
        args.append(a)
        out_shape.append(jax.ShapeDtypeStruct((r, c), BF16))
        out_specs.append(pl.BlockSpec((r // steps, c), lambda i: (i, 0)))
    res = pl.pallas_call(
        functools.partial(_outffn_kernel, final=final, n_cast=len(cast)),
        out_shape=out_shape,
        grid=(steps,),
        in_specs=in_specs,
        out_specs=out_specs,
        scratch_shapes=[pltpu.VMEM((tm, D_FF), BF16)],
        compiler_params=_params("arbitrary"),
        name="outffn",
    )(*args)
    if final:
        return (res[0], None)
    return tuple(res)


def _rope_tables(seq):
    rows = seq // GRID_W
    rowi = jnp.repeat(jnp.arange(rows, dtype=F32), GRID_W)
    coli = jnp.tile(jnp.arange(GRID_W, dtype=F32), rows)
    inv = ROPE_THETA ** (-jnp.arange(0, ROPE_AXIS_DIM, 2, dtype=F32) / ROPE_AXIS_DIM)
    ar = rowi[:, None] * inv[None, :]
    ac = coli[:, None] * inv[None, :]
    ang = jnp.concatenate([ar, ar, ac, ac], axis=-1)
    cos = jnp.tile(jnp.cos(ang), (1, LANES // HEAD_DIM))
    sin = jnp.tile(jnp.sin(ang), (1, LANES // HEAD_DIM))
    first = (jnp.arange(LANES) % ROPE_AXIS_DIM) < ROPE_AXIS_DIM // 2
    return cos, jnp.where(first, -sin, 0.0), jnp.where(first, 0.0, sin)


def kernel(x_prompt, x_sample, cache_k, cache_v, state_hgrn, c, c_ctx, w_mod, b_mod, norm1_g, w_in, lb_raw,
           hgrn_norm_g, q_norm_g, k_norm_g, w_out, norm2_g, w1, w2, final_norm_g):
    bp, sp, _ = x_prompt.shape
    bs, ss, _ = x_sample.shape
    past = cache_k.shape[2]

    cvec = jnp.zeros((MOD_ROWS, D_MODEL), F32).at[0:bs].set(c).at[MOD_ROWS // 2].set(c_ctx)
    mod = _modulation(cvec, w_mod, b_mod).reshape(DEPTH, MOD_ROWS, 1, MOD_W)

    stacked = (w_in, w_out, w1, w2)
    w_in_b, w_out_b, w1_b, w2_b = (w[0].astype(BF16) for w in stacked)
    ck =cache_k.reshape(bs, DEPTH, past, KV_A * HEAD_DIM)
    cv = cache_v.reshape(bs, DEPTH, past, KV_A * HEAD_DIM)
    tabs = _rope_tables(ss)
    gf = final_norm_g.reshape(1, D_MODEL)
    g1 = norm1_g.reshape(DEPTH, 1, D_MODEL)
    g2 = norm2_g.reshape(DEPTH, 1, D_MODEL)
    gn = jnp.tile(hgrn_norm_g, (1, H_R)).reshape(DEPTH, 1, W_R)
    qg = jnp.tile(q_norm_g, (1, H_A)).reshape(DEPTH, 1, W_A)
    kg = jnp.tile(k_norm_g, (1, KV_A)).reshape(DEPTH, 1, LANES)

    xp = x_prompt.reshape(bp * sp, D_MODEL)
    xs = x_sample.reshape(bs * ss, D_MODEL)
    hp = _first_prenorm(xp, mod, g1, sp, True)
    hs = _first_prenorm(xs, mod, g1, ss, False)
    new_s, new_kv = None, None
    for l in range(DEPTH):
        o_r, new_s = _hgrn(hp, w_in_b, lb_raw, gn, l, bp, sp, s_prev=new_s, want_s=True)
        o_a, *new_kv = _attn(hp, w_in_b, qg, kg, bp, sp, l, kv_prev=new_kv)
        xp, hp, *next_w = _outffn(xp, o_r, o_a, mod, g1, g2, w_out_b, w1_b, w2_b, gf, l, sp, True,
                                  cast=stacked if l + 1 < DEPTH else ())

        o_r, _ = _hgrn(hs, w_in_b, lb_raw, gn, l, bs, ss, s0=state_hgrn)
        o_a, = _attn(hs, w_in_b, qg, kg, bs, ss, l, cache=(ck, cv), rope_tabs=tabs)
        xs, hs = _outffn(xs, o_r, o_a, mod, g1, g2, w_out_b, w1_b, w2_b, gf, l, ss, False)
        if next_w:
            w_in_b, w_out_b, w1_b, w2_b = next_w

    y_prompt = xp.reshape(bp, sp, D_MODEL)
    y_sample = xs.reshape(bs, ss, D_MODEL)
    new_k = new_kv[0].reshape(bp, DEPTH, sp, KV_A, HEAD_DIM)
    new_v = new_kv[1].reshape(bp, DEPTH, sp, KV_A, HEAD_DIM)
    return (y_prompt, y_sample, new_k, new_v, new_s)
```

```python
import functools

import jax
import jax.numpy as jnp
import numpy as np
from jax import lax
from jax.experimental import pallas as pl
from jax.experimental.pallas import tpu as pltpu

F32 = jnp.float32
BF16 = jnp.bfloat16

D_MODEL = 1024
DEPTH = 4
GRID_W = 64
D_FF = 4 * D_MODEL
H_R = 4
DK_R = 128
DV_R = 128
W_R = H_R * DV_R
H_A = 8
KV_A = 2
HEAD_DIM = 64
W_A = H_A * HEAD_DIM
ROPE_AXIS_DIM = HEAD_DIM // 2
ROPE_THETA = 10000.0
EPS = 1e-6
LOG2E = 1.4426950408889634
IN_R = 3 * H_R * DK_R + 2 * W_R
IN_A = W_A + 2 * KV_A * HEAD_DIM
MOD_W = 6 * D_MODEL
MOD_ROWS = 8

LANES = 128
SUBLANES = 8
VMEM_LIMIT = 56 * 1024 * 1024

HGRN_CHUNK = 128
FAST_CHUNK = 64
FAST_UNROLL = 8
FAST_LIMIT = 115.0
PROJ_ROWS = 256
HGRN_ROWS = 512
ATTN_QROWS = 256
ATTN_KEYS = 256
ATTN_ROWS = 512
BOUND_SLACK = 1.03
BOUND_LIMIT = 40.0
ATTN_LOOKAHEAD = 2
FFN_CHUNK = 1024
FFN_PARTS = 2
ROW_TILE = 512


def _dot(a, b):
    return jnp.dot(a, b, preferred_element_type=F32)


def _dot_nt(a, b):
    return lax.dot_general(a, b, (((1,), (1,)), ((), ())), preferred_element_type=F32)


def _in_turn(gens):
    active = list(gens)
    while active:
        for gen in list(active):
            if next(gen, "done") == "done":
                active.remove(gen)


def _params(*sem):
    return pltpu.CompilerParams(dimension_semantics=sem, vmem_limit_bytes=VMEM_LIMIT)


def _rms(x):
    return x * lax.rsqrt(jnp.mean(x * x, axis=-1, keepdims=True) + EPS)


def _mod_spec(layer, rows_per_seq, tile, ctx):
    per = rows_per_seq // tile
    row = (lambda i: MOD_ROWS // 2) if ctx else (lambda i: i // per)
    return pl.BlockSpec((None, None, 1, MOD_W), lambda i: (layer, row(i), 0, 0))


def _layer_spec(a, layer):
    return pl.BlockSpec((None,) + a.shape[1:], lambda i: (layer, 0, 0))


def _prenorm(x, shift, scale, g):
    return (_rms(x) * g * (1.0 + scale) + shift).astype(BF16)


def _mod_kernel(c_ref, w_ref, b_ref, o_ref):
    c = c_ref[...]
    s = c / (1.0 + jnp.exp(-c))
    o_ref[...] = _dot(s.astype(BF16), w_ref[...].astype(BF16)) + b_ref[...]


def _modulation(cvec, w_mod, b_mod):
    tn = 1536
    return pl.pallas_call(
        _mod_kernel,
        out_shape=jax.ShapeDtypeStruct((DEPTH, MOD_ROWS, MOD_W), F32),
        grid=(DEPTH, MOD_W // tn),
        in_specs=[
            pl.BlockSpec((MOD_ROWS, D_MODEL), lambda l, j: (0, 0)),
            pl.BlockSpec((None, D_MODEL, tn), lambda l, j: (l, 0, j)),
            pl.BlockSpec((None, 1, tn), lambda l, j: (l, 0, j)),
        ],
        out_specs=pl.BlockSpec((None, MOD_ROWS, tn), lambda l, j: (l, 0, j)),
        compiler_params=_params("arbitrary", "arbitrary"),
        name="modulation",
    )(cvec, w_mod, b_mod.reshape(DEPTH, 1, MOD_W))


def _prenorm_kernel(x_ref, mod_ref, g_ref, o_ref):
    o_ref[...] = _prenorm(x_ref[...], mod_ref[:, 0:D_MODEL], mod_ref[:, D_MODEL:2 * D_MODEL], g_ref[...])


def _first_prenorm(x, mod, g1, rows_per_seq, ctx):
    n = x.shape[0]
    tm = ROW_TILE
    return pl.pallas_call(
        _prenorm_kernel,
        out_shape=jax.ShapeDtypeStruct((n, D_MODEL), BF16),
        grid=(n // tm,),
        in_specs=[pl.BlockSpec((tm, D_MODEL), lambda i: (i, 0)),
                  _mod_spec(0, rows_per_seq, tm, ctx),
                  _layer_spec(g1, 0)],
        out_specs=pl.BlockSpec((tm, D_MODEL), lambda i: (i, 0)),
        compiler_params=_params("arbitrary"),
        name="prenorm",
    )(x, mod, g1)


def _bcast_block_row(b, blk, r):
    n, w = b.shape
    parts = [jnp.broadcast_to(b[k * blk + r:k * blk + r + 1, :], (blk, w)) for k in range(n // blk)]
    return parts[0] if len(parts) == 1 else jnp.concatenate(parts, axis=0)


def _hgrn_kernel(*refs, layer, seq, nseq, has_s0, has_prev, want_s):
    L = HGRN_CHUNK
    n_chunks = seq // L
    n_levels = int(np.log2(L))
    refs = list(refs)
    h_ref, w_ref, lbraw_ref, gn_ref = refs[:4]
    pos = 4
    s0_ref = None
    if has_s0:
        s0_ref = refs[pos]
        pos += 1
    if has_prev:
        pos += 1
    o_ref = refs[pos]
    pos += 1
    s_ref = None
    if want_s:
        s_ref = refs[pos]
        pos += 1
    proj_ref, of_ref, ob_ref, st_ref, code_ref, kg_ref, b_ref = refs[pos:pos + 7]
    q_cols, i_cols, g_cols = (slice(n * W_R, (n + 1) * W_R) for n in (0, 3, 4))
    f_cols = [slice(W_R, 2 * W_R), slice(2 * W_R, 3 * W_R)]

    def project(g):
        rows = slice(g * PROJ_ROWS, (g + 1) * PROJ_ROWS)
        proj_ref[rows, :] = _dot(h_ref[rows, :], w_ref[...])

    if layer > 0:
        raw = lbraw_ref[...]
        e = jnp.exp(raw - jnp.max(raw, axis=0, keepdims=True))
        p = e / jnp.sum(e, axis=0, keepdims=True)
        lb = p[1]
        for j in range(2, layer + 1):
            lb = lb + p[j]
        log_lb = jnp.log(lb)
        log_1m_lb = jnp.log1p(-lb)
        one_m_lb = 1.0 - lb

    def gates(fx, d):
        ls = jnp.minimum(fx, 0.0) - jnp.log(1.0 + jnp.exp(-jnp.abs(fx)))
        k = jnp.exp(ls - fx)
        if layer == 0:
            return ls, k
        a = log_lb[d:d + 1, :]
        c = log_1m_lb[d:d + 1, :] + ls
        lf = jnp.maximum(a, c) + jnp.log(1.0 + jnp.exp(-jnp.abs(a - c)))
        return lf, one_m_lb[d:d + 1, :] * k

    ti = lax.broadcasted_iota(jnp.int32, (L, L), 0)
    si = lax.broadcasted_iota(jnp.int32, (L, L), 1)
    x = ti ^ si
    lv = jnp.zeros((L, L), jnp.int32)
    for j in range(n_levels):
        lv = lv + jnp.where((x >> j) != 0, 1, 0)
    code_ref[0] = jnp.where(ti >= si, lv, -1)
    code_ref[1] = jnp.where(ti <= si, lv, -1)

    def init_states():
        for s in range(nseq):
            for d in range(2):
                for h in range(H_R):
                    if has_s0:
                        st_ref[s, d, h] = s0_ref[s, d, h].T
                    else:
                        st_ref[s, d, h] = jnp.zeros((DV_R, DK_R), F32)

    init_states()

    def cumulative(lf, tri):
        hi = lf.astype(BF16)
        r1 = lf - hi.astype(F32)
        mid = r1.astype(BF16)
        lo = (r1 - mid.astype(F32)).astype(BF16)
        b3 = _dot(tri, jnp.concatenate([hi, mid, lo], axis=1))
        return b3[:, 0:W_R] + b3[:, W_R:2 * W_R] + b3[:, 2 * W_R:3 * W_R]

    Lf = FAST_CHUNK
    n_fast = seq // Lf
    unroll = min(FAST_UNROLL, n_fast)
    ref_row = [Lf // 2 - 1, Lf // 2]
    first_row = [0, Lf - 1]
    last_row = [Lf - 1, 0]
    tf = lax.broadcasted_iota(jnp.int32, (Lf, Lf), 0)
    sf = lax.broadcasted_iota(jnp.int32, (Lf, Lf), 1)
    keep = [tf >= sf, tf <= sf]
    tf2 = lax.broadcasted_iota(jnp.int32, (Lf, 2 * Lf), 0)
    sf2 = lax.broadcasted_iota(jnp.int32, (Lf, 2 * Lf), 1) & (Lf - 1)
    keep2 = [tf2 >= sf2, tf2 <= sf2]

    def prepare(g, worst):
        for i in range(PROJ_ROWS // Lf):
            rows = slice(g * PROJ_ROWS + i * Lf, g * PROJ_ROWS + (i + 1) * Lf)
            q_top = jnp.max(jnp.abs(proj_ref[rows, q_cols]), axis=0, keepdims=True)
            q_bits = jnp.log(jnp.maximum(q_top, 1.0)) * LOG2E
            for d in range(2):
                lf, k = gates(proj_ref[rows, f_cols[d]], d)
                b = cumulative(lf * LOG2E, jnp.where(keep[d], 1.0, 0.0).astype(BF16))
                kg_ref[d, rows, :] = k
                b_ref[d, rows, :] = b
                b_mid = b[ref_row[d]:ref_row[d] + 1, :]
                grow = jnp.maximum(b[first_row[d]:first_row[d] + 1, :] - b_mid,
                                   b_mid - b[last_row[d]:last_row[d] + 1, :])
                worst = jnp.maximum(worst, grow + q_bits)
        return worst

    def fast_direction(d, s, c, out_ref):
        start = s * seq + c * Lf
        rows = slice(start, start + Lf) if isinstance(start, int) else pl.ds(pl.multiple_of(start, Lf), Lf)
        q = proj_ref[rows, q_cols]
        v = proj_ref[rows, i_cols]
        k = kg_ref[d, rows, :]
        b = b_ref[d, rows, :]
        b_mid = b[ref_row[d]:ref_row[d] + 1, :]
        edge = b[last_row[d]:last_row[d] + 1, :]
        q_t = (q * jnp.exp2(b - b_mid)).astype(BF16)
        k_f = k * jnp.exp2(b_mid - b)
        k_t = k_f.astype(BF16)
        k_out = (k_f * jnp.exp2(edge - b_mid)).astype(BF16)
        dec = jnp.exp2(edge)
        from_start = jnp.exp2(b_mid)
        vb = v.astype(BF16)
        zero_s = jnp.zeros((DV_R, DK_R), BF16)
        zero_r = jnp.zeros((Lf, DK_R), BF16)

        def side_by_side(x1, x2, zero):
            return jnp.concatenate([jnp.concatenate([x1, zero], axis=1),
                                    jnp.concatenate([zero, x2], axis=1)], axis=0)

        both = []
        for h in range(0, H_R, 2):
            c1 = slice(h * DK_R, (h + 1) * DK_R)
            c2 = slice((h + 1) * DK_R, (h + 2) * DK_R)
            cp = slice(h * DK_R, (h + 2) * DK_R)
            st1 = st_ref[s, d, h]
            st2 = st_ref[s, d, h + 1]
            rhs = jnp.concatenate([
                side_by_side((st1 * from_start[:, c1]).astype(BF16), (st2 * from_start[:, c2]).astype(BF16), zero_s),
                side_by_side(k_t[:, c1], k_t[:, c2], zero_r)], axis=0)
            both.append(_dot_nt(q_t[:, cp], rhs))
            v_rows = jnp.concatenate([v[:, c1], v[:, c2]], axis=0)
            upd = _dot(v_rows.T.astype(BF16), side_by_side(k_out[:, c1], k_out[:, c2], zero_r))
            st_ref[s, d, h] = st1 * dec[:, c1] + upd[:, 0:DK_R]
            st_ref[s, d, h + 1] = st2 * dec[:, c2] + upd[:, DK_R:2 * DK_R]
        yield
        outs = []
        for n, h in enumerate(range(0, H_R, 2)):
            c1 = slice(h * DK_R, (h + 1) * DK_R)
            c2 = slice((h + 1) * DK_R, (h + 2) * DK_R)
            a = jnp.where(keep2[d], both[n][:, 2 * DV_R:2 * DV_R + 2 * Lf], 0.0).astype(BF16)
            outs.append(both[n][:, 0:2 * DV_R] + _dot(a, side_by_side(vb[:, c1], vb[:, c2], zero_r)))
        out_ref[rows, :] = jnp.concatenate(outs, axis=1)

    def fast_chunks(s, first):
        gens = []
        for i in range(unroll):
            gens.append(fast_direction(0, s, first + i, of_ref))
            gens.append(fast_direction(1, s, n_fast - 1 - (first + i), ob_ref))
        _in_turn(gens)

    n_groups = nseq * seq // PROJ_ROWS
    inline = n_fast == unroll
    worst = jnp.zeros((1, W_R), F32)
    project(0)
    for g in range(n_groups):
        if g + 1 < n_groups:
            project(g + 1)
        worst = prepare(g, worst)
        if inline and ((g + 1) * PROJ_ROWS) % seq == 0:
            fast_chunks((g + 1) * PROJ_ROWS // seq - 1, 0)
    if not inline:
        for s in range(nseq):
            def fast_step(c, carry, s=s):
                fast_chunks(s, c * unroll)
                return carry
            lax.fori_loop(0, n_fast // unroll, fast_step, 0)

    def direction(d, s, c, out_ref):
        rows = pl.ds(pl.multiple_of(s * seq + c * L, L), L)
        row = lax.broadcasted_iota(jnp.int32, (L, W_R), 0)
        code = code_ref[d]
        tri = jnp.where(code >= 0, 1.0, 0.0).astype(BF16)
        q = proj_ref[rows, q_cols]
        v = proj_ref[rows, i_cols]
        lf, k = gates(proj_ref[rows, f_cols[d]], d)
        lf = lf * LOG2E
        b = cumulative(lf, tri)
        yield

        vb = v.astype(BF16)
        a_h = [None] * H_R
        for j in range(1, n_levels + 1):
            half = 1 << (j - 1)
            blk = 2 * half
            u = row & (blk - 1)
            qside = (u >= half) if d == 0 else (u < half)
            if j == 1:
                e_j = jnp.where(qside, lf, 0.0)
            elif j == 2:
                up = pltpu.roll(lf, L - 1, axis=0)
                dn = pltpu.roll(lf, 1, axis=0)
                if d == 0:
                    e_j = jnp.where(u == 0, up, jnp.where(u == 1, 0.0, jnp.where(u == 2, lf, lf + dn)))
                else:
                    e_j = jnp.where(u == 0, lf + up, jnp.where(u == 1, lf, jnp.where(u == 2, 0.0, dn)))
            else:
                diff = b - _bcast_block_row(b, blk, half - 1 + d)
                e_j = jnp.where(qside, diff, -diff)
            z = (jnp.where(qside, q, k) * jnp.exp2(e_j)).astype(BF16)
            for h in range(H_R):
                cs = slice(h * DK_R, (h + 1) * DK_R)
                p_j = _dot_nt(z[:, cs], z[:, cs])
                a_h[h] = jnp.where(code == j, p_j, 0.0 if a_h[h] is None else a_h[h])
            yield

        edge = b[L - 1:L, :] if d == 0 else b[0:1, :]
        q_in = (q * jnp.exp2(b)).astype(BF16)
        k_out = (k * jnp.exp2(edge - b)).astype(BF16)
        dec = jnp.exp2(edge)
        qk = q * k
        outs = []
        for h in range(H_R):
            cs = slice(h * DK_R, (h + 1) * DK_R)
            st = st_ref[s, d, h]
            o = _dot(a_h[h].astype(BF16), vb[:, cs]) + _dot_nt(q_in[:, cs], st.astype(BF16))
            o = o + jnp.sum(qk[:, cs], axis=-1, keepdims=True) * v[:, cs]
            st_ref[s, d, h] = st * dec[:, cs] + _dot(v[:, cs].T.astype(BF16), k_out[:, cs])
            outs.append(o)
        out_ref[rows, :] = jnp.concatenate(outs, axis=1)

    def step(i, carry):
        s, c = i // n_chunks, i % n_chunks
        _in_turn([direction(0, s, c, of_ref), direction(1, s, n_chunks - 1 - c, ob_ref)])
        return carry

    @pl.when(jnp.max(worst) > FAST_LIMIT)
    def _():
        init_states()
        lax.fori_loop(0, nseq * n_chunks, step, 0)

    def finish(c, carry):
        rows = pl.ds(pl.multiple_of(c * L, L), L)
        o = of_ref[rows, :] + ob_ref[rows, :]
        g = proj_ref[rows, g_cols]
        parts = [_rms(o[:, h * DV_R:(h + 1) * DV_R]) for h in range(H_R)]
        y = jnp.concatenate(parts, axis=1) * gn_ref[...]
        o_ref[rows, :] = y * (g / (1.0 + jnp.exp(-g)))
        return carry

    lax.fori_loop(0, nseq * n_chunks, finish, 0)

    if want_s:
        for s in range(nseq):
            own = s_ref.at[s] if has_prev else s_ref.at[s, layer]
            for d in range(2):
                for h in range(H_R):
                    own[d, h] = st_ref[s, d, h].T
            if not has_prev:
                for other in range(DEPTH):
                    if other != layer:
                        s_ref[s, other] = jnp.zeros((2, H_R, DK_R, DV_R), F32)


def _hgrn(hb, w_in, lb_raw, gn, layer, batch, seq, s0=None, s_prev=None, want_s=False):
    has_s0 = s0 is not None
    nseq = max(1, HGRN_ROWS // seq)
    rows = nseq * seq
    state_spec = pl.BlockSpec((nseq, None, 2, H_R, DK_R, DV_R), lambda b: (b, layer, 0, 0, 0, 0))
    in_specs = [pl.BlockSpec((rows, D_MODEL), lambda b: (b, 0)),
                pl.BlockSpec((D_MODEL, IN_R), lambda b: (0, 0)),
                pl.BlockSpec((DEPTH, 2, W_R), lambda b: (0, 0, 0)),
                _layer_spec(gn, layer)]
    args = [hb, w_in, lb_raw, gn]
    if has_s0:
        in_specs.append(state_spec)
        args.append(s0)
    out_shape = [jax.ShapeDtypeStruct((batch * seq, W_R), F32)]
    out_specs = [pl.BlockSpec((rows, W_R), lambda b: (b, 0))]
    aliases = {}
    if want_s:
        out_shape.append(jax.ShapeDtypeStruct((batch, DEPTH, 2, H_R, DK_R, DV_R), F32))
        if s_prev is not None:
            out_specs.append(state_spec)
            aliases[len(args)] = 1
            in_specs.append(pl.BlockSpec(memory_space=pl.ANY))
            args.append(s_prev)
        else:
            out_specs.append(pl.BlockSpec((nseq, DEPTH, 2, H_R, DK_R, DV_R), lambda b: (b, 0, 0, 0, 0, 0)))
    res = pl.pallas_call(
        functools.partial(_hgrn_kernel, layer=layer, seq=seq, nseq=nseq, has_s0=has_s0,
                          has_prev=bool(aliases), want_s=want_s),
        out_shape=out_shape,
        grid=(batch // nseq,),
        in_specs=in_specs,
        out_specs=out_specs,
        scratch_shapes=[pltpu.VMEM((rows, IN_R), F32),
                        pltpu.VMEM((rows, W_R), F32), pltpu.VMEM((rows, W_R), F32),
                        pltpu.VMEM((nseq, 2, H_R, DV_R, DK_R), F32),
                        pltpu.VMEM((2, HGRN_CHUNK, HGRN_CHUNK), jnp.int32),
                        pltpu.VMEM((2, rows, W_R), F32), pltpu.VMEM((2, rows, W_R), F32)],
        input_output_aliases=aliases,
        compiler_params=_params("arbitrary"),
        name="hgrn",
    )(*args)
    return res if want_s else (res[0], None)


def _head_mean_square(x):
    w = x.shape[1]
    shift = HEAD_DIM.bit_length() - 1
    r = lax.broadcasted_iota(jnp.int32, (w, w), 0) >> shift
    c = lax.broadcasted_iota(jnp.int32, (w, w), 1) >> shift
    ones = jnp.where(r == c, 1.0, 0.0).astype(BF16)
    return _dot((x * x).astype(BF16), ones) * (1.0 / HEAD_DIM)


def _attn_kernel(*refs, seq, nseq, past, n_prev, layer):
    refs = list(refs)
    h_ref, wq0_ref, wq1_ref, wkv_ref, qg_ref, kg_ref = refs[:6]
    pos = 6
    if past:
        ck_ref, cv_ref, cos_ref, sa_ref, sb_ref = refs[pos:pos + 5]
        pos += 5
        o_ref = refs[pos]
        pos += 1
    else:
        pos += n_prev
        o_ref, kn_ref, vn_ref = refs[pos:pos + 3]
        pos += 3
    qs_ref, ks_ref, vt_ref = refs[pos:pos + 3]

    def rope(x):
        up = pltpu.roll(x, LANES - ROPE_AXIS_DIM // 2, axis=1)
        dn = pltpu.roll(x, ROPE_AXIS_DIM // 2, axis=1)
        return x * cos_ref[...] + up * sa_ref[...] + dn * sb_ref[...]

    hb = h_ref[...]
    q = jnp.concatenate([_dot(hb, wq0_ref[...]), _dot(hb, wq1_ref[...])], axis=1)
    kv = _dot(hb, wkv_ref[...])
    k = kv[:, 0:LANES]
    v = kv[:, LANES:2 * LANES]
    qn = q * lax.rsqrt(_head_mean_square(q) + EPS) * qg_ref[...]
    kn = k * lax.rsqrt(_head_mean_square(k) + EPS) * kg_ref[...]
    scale = HEAD_DIM ** -0.5 * LOG2E
    low = lax.broadcasted_iota(jnp.int32, (nseq * seq, LANES), 1) < HEAD_DIM
    pairs_per_kv = (W_A // LANES) // KV_A
    q_top2 = HEAD_DIM * scale * scale * jnp.max(qg_ref[...] * qg_ref[...])
    k_new_top2 = HEAD_DIM * jnp.max(kg_ref[...] * kg_ref[...])
    for p in range(W_A // LANES):
        cs = slice(p * LANES, (p + 1) * LANES)
        qp = (rope(qn[:, cs]) if past else qn[:, cs]) * scale
        even = jnp.where(low, qp, 0.0)
        odd = jnp.where(low, 0.0, qp)
        if p // pairs_per_kv == 0:
            odd = pltpu.roll(odd, HEAD_DIM, axis=1)
        else:
            even = pltpu.roll(even, HEAD_DIM, axis=1)
        qs_ref[0, :, cs] = even.astype(BF16)
        qs_ref[1, :, cs] = odd.astype(BF16)
    k_top2 = k_new_top2
    if past:
        k_top2 = jnp.maximum(k_top2, jnp.max(_head_mean_square(ck_ref[...])) * HEAD_DIM)
        ks_ref[0] = jnp.concatenate([ck_ref[...], rope(kn)], axis=0).astype(BF16)
        vt_ref[0] = jnp.concatenate([cv_ref[...], v], axis=0).T.astype(BF16)
    else:
        for i in range(nseq):
            rs = slice(i * seq, (i + 1) * seq)
            ks_ref[i] = kn[rs, :].astype(BF16)
            vt_ref[i] = v[rs, :].T.astype(BF16)
            if n_prev:
                kn_ref[i] = kn[rs, :]
                vn_ref[i] = v[rs, :]
            else:
                for other in range(DEPTH):
                    kn_ref[i, other] = kn[rs, :] if other == layer else jnp.zeros((seq, LANES), F32)
                    vn_ref[i, other] = v[rs, :] if other == layer else jnp.zeros((seq, LANES), F32)

    tq = min(ATTN_QROWS, seq)
    total = past + seq

    kb = ATTN_KEYS if total % ATTN_KEYS == 0 else total
    n_blocks = total // kb
    n_tiles = seq // tq
    seqs = range(nseq) if n_tiles == 1 else range(1)
    steps = [(i, p, kc) for i in seqs for p in range(W_A // LANES) for kc in range(n_blocks)]

    worst = jnp.sqrt(q_top2 * k_top2) * BOUND_SLACK

    def tile(r, carry, bounded):
        def q_rows(i):
            start = i * seq + r * tq
            return slice(start, start + tq) if isinstance(start, int) else pl.ds(pl.multiple_of(start, tq), tq)

        def q_pair(i, p):
            cs = slice(p * LANES, (p + 1) * LANES)
            return jnp.concatenate([qs_ref[0, q_rows(i), cs], qs_ref[1, q_rows(i), cs]], axis=0)

        def scores(step):
            i, p, kc = step
            return _dot_nt(ks_ref[i, kc * kb:(kc + 1) * kb, :], q_pair(i, p))

        ahead = [scores(st) for st in steps[:ATTN_LOOKAHEAD]]
        for n, (i, p, kc) in enumerate(steps):
            s = ahead.pop(0)
            if n + ATTN_LOOKAHEAD < len(steps):
                ahead.append(scores(steps[n + ATTN_LOOKAHEAD]))
            kh = p // pairs_per_kv
            v_t = vt_ref[i, kh * HEAD_DIM:(kh + 1) * HEAD_DIM, kc * kb:(kc + 1) * kb]
            if bounded:
                pexp = jnp.exp2(s - worst)
                part = jnp.sum(pexp, axis=0, keepdims=True)
                prod = _dot(v_t, pexp.astype(BF16))
                den = part if kc == 0 else den + part
                o2 = prod if kc == 0 else o2 + prod
            else:
                s_max = jnp.max(s, axis=0, keepdims=True)
                if kc == 0:
                    m = s_max
                    pexp = jnp.exp2(s - m)
                    den = jnp.sum(pexp, axis=0, keepdims=True)
                    o2 = _dot(v_t, pexp.astype(BF16))
                else:
                    m_new = jnp.maximum(m, s_max)
                    alpha = jnp.exp2(m - m_new)
                    pexp = jnp.exp2(s - m_new)
                    den = den * alpha + jnp.sum(pexp, axis=0, keepdims=True)
                    o2 = o2 * alpha + _dot(v_t, pexp.astype(BF16))
                    m = m_new
            if kc == n_blocks - 1:
                o2 = o2 / den
                o_ref[q_rows(i), p * LANES:(p + 1) * LANES] = jnp.concatenate([o2[:, 0:tq], o2[:, tq:2 * tq]], axis=0).T
        return carry

    def tiles(bounded):
        if n_tiles == 1:
            tile(0, 0, bounded)
        else:
            assert nseq == 1
            lax.fori_loop(0, n_tiles, functools.partial(tile, bounded=bounded), 0)

    tiles(True)

    @pl.when(worst > BOUND_LIMIT)
    def _():
        tiles(False)


def _attn(hb, w_in, qg, kg, batch, seq, layer, cache=None, rope_tabs=None, kv_prev=None):
    past = 0 if cache is None else cache[0].shape[2]
    wide = 2 * LANES
    w_specs = [pl.BlockSpec((D_MODEL, wide), lambda b, j=j: (0, IN_R // wide + j))
               for j in range(IN_A // wide)]
    nseq = max(1, ATTN_ROWS // seq)
    rows = nseq * seq
    in_specs = [pl.BlockSpec((rows, D_MODEL), lambda b: (b, 0)), *w_specs,
                _layer_spec(qg, layer), _layer_spec(kg, layer)]
    args = [hb, w_in, w_in, w_in, qg, kg]
    out_shape = [jax.ShapeDtypeStruct((batch * seq, W_A), F32)]
    out_specs = [pl.BlockSpec((rows, W_A), lambda b: (b, 0))]
    aliases = {}
    if past:
        ck, cv = cache
        cspec = pl.BlockSpec((None, None, past, LANES), lambda b: (b, layer, 0, 0))
        tspec = pl.BlockSpec((seq, LANES), lambda b: (0, 0))
        in_specs += [cspec, cspec, tspec, tspec, tspec]
        args += [ck, cv, *rope_tabs]
    else:
        for i in range(2):
            out_shape.append(jax.ShapeDtypeStruct((batch, DEPTH, seq, LANES), F32))
            if kv_prev is not None:
                out_specs.append(pl.BlockSpec((nseq, None, seq, LANES), lambda b: (b, layer, 0, 0)))
                aliases[len(args)] = 1 + i
                in_specs.append(pl.BlockSpec(memory_space=pl.ANY))
                args.append(kv_prev[i])
            else:
                out_specs.append(pl.BlockSpec((nseq, DEPTH, seq, LANES), lambda b: (b, 0, 0, 0)))
    res = pl.pallas_call(
        functools.partial(_attn_kernel, seq=seq, nseq=nseq, past=past, n_prev=len(aliases), layer=layer),
        out_shape=out_shape,
        grid=(batch // nseq,),
        in_specs=in_specs,
        out_specs=out_specs,
        scratch_shapes=[pltpu.VMEM((2, rows, W_A), BF16),
                        pltpu.VMEM((nseq, past + seq, LANES), BF16),
                        pltpu.VMEM((nseq, LANES, past + seq), BF16)],
        input_output_aliases=aliases,
        compiler_params=_params("arbitrary"),
        name="attn",
    )(*args)
    return res


def _outffn_kernel(*refs, final, n_cast):
    refs = list(refs)
    x_ref, or_ref, oa_ref, mod_ref, g2_ref, wout_ref, w1_ref, w2_ref, last_ref = refs[:9]
    pos = 9
    if not final:
        nmod_ref = refs[pos]
        pos += 1
    n_out = 1 if final else 2
    for src, dst in zip(refs[pos:pos + n_cast], refs[pos + n_cast + n_out:pos + 2 * n_cast + n_out]):
        dst[...] = src[...].astype(BF16)
    pos += n_cast
    o_ref = refs[pos]
    gate1 = mod_ref[:, 2 * D_MODEL:3 * D_MODEL]
    shift2 = mod_ref[:, 3 * D_MODEL:4 * D_MODEL]
    scale2 = mod_ref[:, 4 * D_MODEL:5 * D_MODEL]
    gate2 = mod_ref[:, 5 * D_MODEL:6 * D_MODEL]
    hid_ref = refs[-1]

    def rows_part(rs):
        mix = _dot(or_ref[rs, :].astype(BF16), wout_ref[0:W_R, :]) + _dot(oa_ref[rs, :].astype(BF16), wout_ref[W_R:W_R + W_A, :])
        yield
        x1 = x_ref[rs, :] + gate1 * mix
        h2 = _prenorm(x1, shift2, scale2, g2_ref[...])
        for j in range(D_FF // FFN_CHUNK):
            cs = slice(j * FFN_CHUNK, (j + 1) * FFN_CHUNK)
            hid = jnp.maximum(_dot(h2, w1_ref[:, cs]), 0.0)
            hid_ref[rs, cs] = (hid * hid).astype(BF16)
        yield
        x2 = x1 + gate2 * _dot(hid_ref[rs, :], w2_ref[...])
        yield
        if final:
            o_ref[rs, :] = _rms(x2) * last_ref[...]
        else:
            o_ref[rs, :] = x2
            hn_ref = refs[pos + 1]
            hn_ref[rs, :] = _prenorm(x2, nmod_ref[:, 0:D_MODEL], nmod_ref[:, D_MODEL:2 * D_MODEL], last_ref[...])

    n_rows = x_ref.shape[0]
    part = n_rows // FFN_PARTS
    _in_turn([rows_part(slice(i * part, (i + 1) * part)) for i in range(FFN_PARTS)])


def _outffn(x, o_r, o_a, mod, g1, g2, wout, w1, w2, gf, layer, rows_per_seq, ctx, cast=()):
    n = x.shape[0]
    tm = ROW_TILE
    steps = n // tm
    final = layer == DEPTH - 1
    row = lambda w: pl.BlockSpec((tm, w), lambda i: (i, 0))
    whole = lambda a: pl.BlockSpec(a.shape, lambda i: (0, 0))
    in_specs = [row(D_MODEL), row(W_R), row(W_A),
                _mod_spec(layer, rows_per_seq, tm, ctx),
                _layer_spec(g2, layer), whole(wout), whole(w1), whole(w2)]
    args = [x, o_r, o_a, mod, g2, wout, w1, w2]
    out_shape = [jax.ShapeDtypeStruct((n, D_MODEL), F32)]
    out_specs = [row(D_MODEL)]
    if final:
        in_specs.append(pl.BlockSpec(gf.shape, lambda i: (0, 0)))
        args.append(gf)
    else:
        in_specs += [_layer_spec(g1, layer + 1), _mod_spec(layer + 1, rows_per_seq, tm, ctx)]
        args += [g1, mod]
        out_shape.append(jax.ShapeDtypeStruct((n, D_MODEL), BF16))
        out_specs.append(row(D_MODEL))
    for a in cast:
        r, c = a.shape[1:]
        in_specs.append(pl.BlockSpec((None, r // steps, c), lambda i: (layer + 1, i, 0)))
        args.append(a)
        out_shape.append(jax.ShapeDtypeStruct((r, c), BF16))
        out_specs.append(pl.BlockSpec((r // steps, c), lambda i: (i, 0)))
    res = pl.pallas_call(
        functools.partial(_outffn_kernel, final=final, n_cast=len(cast)),
        out_shape=out_shape,
        grid=(steps,),
        in_specs=in_specs,
        out_specs=out_specs,
        scratch_shapes=[pltpu.VMEM((tm, D_FF), BF16)],
        compiler_params=_params("arbitrary"),
        name="outffn",
    )(*args)
    if final:
        return (res[0], None)
    return tuple(res)


def _rope_tables(seq):
    rows = seq // GRID_W
    rowi = jnp.repeat(jnp.arange(rows, dtype=F32), GRID_W)
    coli = jnp.tile(jnp.arange(GRID_W, dtype=F32), rows)
    inv = ROPE_THETA ** (-jnp.arange(0, ROPE_AXIS_DIM, 2, dtype=F32) / ROPE_AXIS_DIM)
    ar = rowi[:, None] * inv[None, :]
    ac = coli[:, None] * inv[None, :]
    ang = jnp.concatenate([ar, ar, ac, ac], axis=-1)
    cos = jnp.tile(jnp.cos(ang), (1, LANES // HEAD_DIM))
    sin = jnp.tile(jnp.sin(ang), (1, LANES // HEAD_DIM))
    first = (jnp.arange(LANES) % ROPE_AXIS_DIM) < ROPE_AXIS_DIM // 2
    return cos, jnp.where(first, -sin, 0.0), jnp.where(first, 0.0, sin)


def kernel(x_prompt, x_sample, cache_k, cache_v, state_hgrn, c, c_ctx, w_mod, b_mod, norm1_g, w_in, lb_raw,
           hgrn_norm_g, q_norm_g, k_norm_g, w_out, norm2_g, w1, w2, final_norm_g):
    bp, sp, _ = x_prompt.shape
    bs, ss, _ = x_sample.shape
    past = cache_k.shape[2]

    cvec = jnp.zeros((MOD_ROWS, D_MODEL), F32).at[0:bs].set(c).at[MOD_ROWS // 2].set(c_ctx)
    mod = _modulation(cvec, w_mod, b_mod).reshape(DEPTH, MOD_ROWS, 1, MOD_W)

    stacked = (w_in, w_out, w1, w2)
    w_in_b, w_out_b, w1_b, w2_b = (w[0].astype(BF16) for w in stacked)
    ck =cache_k.reshape(bs, DEPTH, past, KV_A * HEAD_DIM)
    cv = cache_v.reshape(bs, DEPTH, past, KV_A * HEAD_DIM)
    tabs = _rope_tables(ss)
    gf = final_norm_g.reshape(1, D_MODEL)
    g1 = norm1_g.reshape(DEPTH, 1, D_MODEL)
    g2 = norm2_g.reshape(DEPTH, 1, D_MODEL)
    gn = jnp.tile(hgrn_norm_g, (1, H_R)).reshape(DEPTH, 1, W_R)
    qg = jnp.tile(q_norm_g, (1, H_A)).reshape(DEPTH, 1, W_A)
    kg = jnp.tile(k_norm_g, (1, KV_A)).reshape(DEPTH, 1, LANES)

    xp = x_prompt.reshape(bp * sp, D_MODEL)
    xs = x_sample.reshape(bs * ss, D_MODEL)
    hp = _first_prenorm(xp, mod, g1, sp, True)
    hs = _first_prenorm(xs, mod, g1, ss, False)
    new_s, new_kv = None, None
    for l in range(DEPTH):
        o_r, new_s = _hgrn(hp, w_in_b, lb_raw, gn, l, bp, sp, s_prev=new_s, want_s=True)
        o_a, *new_kv = _attn(hp, w_in_b, qg, kg, bp, sp, l, kv_prev=new_kv)
        xp, hp, *next_w = _outffn(xp, o_r, o_a, mod, g1, g2, w_out_b, w1_b, w2_b, gf, l, sp, True,
                                  cast=stacked if l + 1 < DEPTH else ())

        o_r, _ = _hgrn(hs, w_in_b, lb_raw, gn, l, bs, ss, s0=state_hgrn)
        o_a, = _attn(hs, w_in_b, qg, kg, bs, ss, l, cache=(ck, cv), rope_tabs=tabs)
        xs, hs = _outffn(xs, o_r, o_a, mod, g1, g2, w_out_b, w1_b, w2_b, gf, l, ss, False)
        if next_w:
            w_in_b, w_out_b, w1_b, w2_b = next_w

    y_prompt = xp.reshape(bp, sp, D_MODEL)
    y_sample = xs.reshape(bs, ss, D_MODEL)
    new_k = new_kv[0].reshape(bp, DEPTH, sp, KV_A, HEAD_DIM)
    new_v = new_kv[1].reshape(bp, DEPTH, sp, KV_A, HEAD_DIM)
    return (y_prompt, y_sample, new_k, new_v, new_s)
```

```python
import functools

import jax
import jax.numpy as jnp
import numpy as np
from jax import lax
from jax.experimental import pallas as pl
from jax.experimental.pallas import tpu as pltpu

F32 = jnp.float32
BF16 = jnp.bfloat16

D_MODEL = 1024
DEPTH = 4
GRID_W = 64
D_FF = 4 * D_MODEL
H_R = 4
DK_R = 128
DV_R = 128
W_R = H_R * DV_R
H_A = 8
KV_A = 2
HEAD_DIM = 64
W_A = H_A * HEAD_DIM
ROPE_AXIS_DIM = HEAD_DIM // 2
ROPE_THETA = 10000.0
EPS = 1e-6
LOG2E = 1.4426950408889634
IN_R = 3 * H_R * DK_R + 2 * W_R
IN_A = W_A + 2 * KV_A * HEAD_DIM
MOD_W = 6 * D_MODEL
MOD_ROWS = 8

LANES = 128
VMEM_LIMIT = 56 * 1024 * 1024

HGRN_CHUNK = 128
FAST_CHUNK = 64
FAST_UNROLL = 8
FAST_LIMIT = 115.0
PROJ_ROWS = 256
HGRN_ROWS = 512
ATTN_QROWS = 256
ATTN_KEYS = 256
ATTN_ROWS = 512
BOUND_SLACK = 1.03
BOUND_LIMIT = 40.0
ATTN_TILES = 2
ATTN_LOOKAHEAD = 2
FFN_CHUNK = 1024
FFN_PARTS = 2
ROW_TILE = 512


def _dot(a, b):
    return jnp.dot(a, b, preferred_element_type=F32)


def _dot_nt(a, b):
    return lax.dot_general(a, b, (((1,), (1,)), ((), ())), preferred_element_type=F32)


def _in_turn(gens):
    active = list(gens)
    while active:
        for gen in list(active):
            if next(gen, "done") == "done":
                active.remove(gen)


def _params(*sem):
    return pltpu.CompilerParams(dimension_semantics=sem, vmem_limit_bytes=VMEM_LIMIT)


def _rms(x):
    return x * lax.rsqrt(jnp.mean(x * x, axis=-1, keepdims=True) + EPS)


def _mod_spec(layer, rows_per_seq, tile, ctx):
    per = rows_per_seq // tile
    row = (lambda i: MOD_ROWS // 2) if ctx else (lambda i: i // per)
    return pl.BlockSpec((None, None, 1, MOD_W), lambda i: (layer, row(i), 0, 0))


def _layer_spec(a, layer):
    return pl.BlockSpec((None,) + a.shape[1:], lambda i: (layer, 0, 0))


def _prenorm(x, shift, scale, g):
    return (_rms(x) * g * (1.0 + scale) + shift).astype(BF16)


def _mod_kernel(c_ref, w_ref, b_ref, o_ref):
    c = c_ref[...]
    s = c / (1.0 + jnp.exp(-c))
    o_ref[...] = _dot(s.astype(BF16), w_ref[...].astype(BF16)) + b_ref[...]


def _modulation(cvec, w_mod, b_mod):
    tn = 1536
    return pl.pallas_call(
        _mod_kernel,
        out_shape=jax.ShapeDtypeStruct((DEPTH, MOD_ROWS, MOD_W), F32),
        grid=(DEPTH, MOD_W // tn),
        in_specs=[
            pl.BlockSpec((MOD_ROWS, D_MODEL), lambda l, j: (0, 0)),
            pl.BlockSpec((None, D_MODEL, tn), lambda l, j: (l, 0, j)),
            pl.BlockSpec((None, 1, tn), lambda l, j: (l, 0, j)),
        ],
        out_specs=pl.BlockSpec((None, MOD_ROWS, tn), lambda l, j: (l, 0, j)),
        compiler_params=_params("arbitrary", "arbitrary"),
        name="modulation",
    )(cvec, w_mod, b_mod.reshape(DEPTH, 1, MOD_W))


def _prenorm_kernel(x_ref, mod_ref, g_ref, o_ref):
    o_ref[...] = _prenorm(x_ref[...], mod_ref[:, 0:D_MODEL], mod_ref[:, D_MODEL:2 * D_MODEL], g_ref[...])


def _first_prenorm(x, mod, g1, rows_per_seq, ctx):
    n = x.shape[0]
    tm = ROW_TILE
    return pl.pallas_call(
        _prenorm_kernel,
        out_shape=jax.ShapeDtypeStruct((n, D_MODEL), BF16),
        grid=(n // tm,),
        in_specs=[pl.BlockSpec((tm, D_MODEL), lambda i: (i, 0)),
                  _mod_spec(0, rows_per_seq, tm, ctx),
                  _layer_spec(g1, 0)],
        out_specs=pl.BlockSpec((tm, D_MODEL), lambda i: (i, 0)),
        compiler_params=_params("arbitrary"),
        name="prenorm",
    )(x, mod, g1)


def _bcast_block_row(b, blk, r):
    n, w = b.shape
    parts = [jnp.broadcast_to(b[k * blk + r:k * blk + r + 1, :], (blk, w)) for k in range(n // blk)]
    return parts[0] if len(parts) == 1 else jnp.concatenate(parts, axis=0)


def _hgrn_kernel(*refs, layer, seq, nseq, has_s0, has_prev, want_s):
    L = HGRN_CHUNK
    n_chunks = seq // L
    n_levels = int(np.log2(L))
    refs = list(refs)
    h_ref, w_ref, lbraw_ref, gn_ref = refs[:4]
    pos = 4
    s0_ref = None
    if has_s0:
        s0_ref = refs[pos]
        pos += 1
    if has_prev:
        pos += 1
    o_ref = refs[pos]
    pos += 1
    s_ref = None
    if want_s:
        s_ref = refs[pos]
        pos += 1
    proj_ref, of_ref, ob_ref, st_ref, code_ref, kg_ref, b_ref = refs[pos:pos + 7]
    q_cols, i_cols, g_cols = (slice(n * W_R, (n + 1) * W_R) for n in (0, 3, 4))
    f_cols = [slice(W_R, 2 * W_R), slice(2 * W_R, 3 * W_R)]

    def project(g):
        rows = slice(g * PROJ_ROWS, (g + 1) * PROJ_ROWS)
        proj_ref[rows, :] = _dot(h_ref[rows, :], w_ref[...])

    if layer > 0:
        raw = lbraw_ref[...]
        e = jnp.exp(raw - jnp.max(raw, axis=0, keepdims=True))
        p = e / jnp.sum(e, axis=0, keepdims=True)
        lb = p[1]
        for j in range(2, layer + 1):
            lb = lb + p[j]
        log_lb = jnp.log(lb)
        log_1m_lb = jnp.log1p(-lb)
        one_m_lb = 1.0 - lb

    def gates(fx, d):
        ls = jnp.minimum(fx, 0.0) - jnp.log(1.0 + jnp.exp(-jnp.abs(fx)))
        k = jnp.exp(ls - fx)
        if layer == 0:
            return ls, k
        a = log_lb[d:d + 1, :]
        c = log_1m_lb[d:d + 1, :] + ls
        lf = jnp.maximum(a, c) + jnp.log(1.0 + jnp.exp(-jnp.abs(a - c)))
        return lf, one_m_lb[d:d + 1, :] * k

    ti = lax.broadcasted_iota(jnp.int32, (L, L), 0)
    si = lax.broadcasted_iota(jnp.int32, (L, L), 1)
    x = ti ^ si
    lv = jnp.zeros((L, L), jnp.int32)
    for j in range(n_levels):
        lv = lv + jnp.where((x >> j) != 0, 1, 0)
    code_ref[0] = jnp.where(ti >= si, lv, -1)
    code_ref[1] = jnp.where(ti <= si, lv, -1)

    def init_states():
        for s in range(nseq):
            for d in range(2):
                for h in range(H_R):
                    if has_s0:
                        st_ref[s, d, h] = s0_ref[s, d, h].T
                    else:
                        st_ref[s, d, h] = jnp.zeros((DV_R, DK_R), F32)

    init_states()

    def cumulative(lf, tri):
        hi = lf.astype(BF16)
        r1 = lf - hi.astype(F32)
        mid = r1.astype(BF16)
        lo = (r1 - mid.astype(F32)).astype(BF16)
        b3 = _dot(tri, jnp.concatenate([hi, mid, lo], axis=1))
        return b3[:, 0:W_R] + b3[:, W_R:2 * W_R] + b3[:, 2 * W_R:3 * W_R]

    Lf = FAST_CHUNK
    n_fast = seq // Lf
    unroll = min(FAST_UNROLL, n_fast)
    ref_row = [Lf // 2 - 1, Lf // 2]
    first_row = [0, Lf - 1]
    last_row = [Lf - 1, 0]
    tf = lax.broadcasted_iota(jnp.int32, (Lf, Lf), 0)
    sf = lax.broadcasted_iota(jnp.int32, (Lf, Lf), 1)
    keep = [tf >= sf, tf <= sf]
    tf2 = lax.broadcasted_iota(jnp.int32, (Lf, 2 * Lf), 0)
    sf2 = lax.broadcasted_iota(jnp.int32, (Lf, 2 * Lf), 1) & (Lf - 1)
    keep2 = [tf2 >= sf2, tf2 <= sf2]

    def prepare(g, worst):
        for i in range(PROJ_ROWS // Lf):
            rows = slice(g * PROJ_ROWS + i * Lf, g * PROJ_ROWS + (i + 1) * Lf)
            q_top = jnp.max(jnp.abs(proj_ref[rows, q_cols]), axis=0, keepdims=True)
            q_bits = jnp.log(jnp.maximum(q_top, 1.0)) * LOG2E
            for d in range(2):
                lf, k = gates(proj_ref[rows, f_cols[d]], d)
                b = cumulative(lf * LOG2E, jnp.where(keep[d], 1.0, 0.0).astype(BF16))
                kg_ref[d, rows, :] = k
                b_ref[d, rows, :] = b
                b_mid = b[ref_row[d]:ref_row[d] + 1, :]
                grow = jnp.maximum(b[first_row[d]:first_row[d] + 1, :] - b_mid,
                                   b_mid - b[last_row[d]:last_row[d] + 1, :])
                worst = jnp.maximum(worst, grow + q_bits)
        return worst

    def fast_direction(d, s, c, out_ref):
        start = s * seq + c * Lf
        rows = slice(start, start + Lf) if isinstance(start, int) else pl.ds(pl.multiple_of(start, Lf), Lf)
        q = proj_ref[rows, q_cols]
        v = proj_ref[rows, i_cols]
        k = kg_ref[d, rows, :]
        b = b_ref[d, rows, :]
        b_mid = b[ref_row[d]:ref_row[d] + 1, :]
        edge = b[last_row[d]:last_row[d] + 1, :]
        q_t = (q * jnp.exp2(b - b_mid)).astype(BF16)
        k_f = k * jnp.exp2(b_mid - b)
        k_t = k_f.astype(BF16)
        k_out = (k_f * jnp.exp2(edge - b_mid)).astype(BF16)
        dec = jnp.exp2(edge)
        from_start = jnp.exp2(b_mid)
        vb = v.astype(BF16)
        zero_s = jnp.zeros((DV_R, DK_R), BF16)
        zero_r = jnp.zeros((Lf, DK_R), BF16)

        def side_by_side(x1, x2, zero):
            return jnp.concatenate([jnp.concatenate([x1, zero], axis=1),
                                    jnp.concatenate([zero, x2], axis=1)], axis=0)

        both = []
        for h in range(0, H_R, 2):
            c1 = slice(h * DK_R, (h + 1) * DK_R)
            c2 = slice((h + 1) * DK_R, (h + 2) * DK_R)
            cp = slice(h * DK_R, (h + 2) * DK_R)
            st1 = st_ref[s, d, h]
            st2 = st_ref[s, d, h + 1]
            rhs = jnp.concatenate([
                side_by_side((st1 * from_start[:, c1]).astype(BF16), (st2 * from_start[:, c2]).astype(BF16), zero_s),
                side_by_side(k_t[:, c1], k_t[:, c2], zero_r)], axis=0)
            both.append(_dot_nt(q_t[:, cp], rhs))
            v_rows = jnp.concatenate([v[:, c1], v[:, c2]], axis=0)
            upd = _dot(v_rows.T.astype(BF16), side_by_side(k_out[:, c1], k_out[:, c2], zero_r))
            st_ref[s, d, h] = st1 * dec[:, c1] + upd[:, 0:DK_R]
            st_ref[s, d, h + 1] = st2 * dec[:, c2] + upd[:, DK_R:2 * DK_R]
        yield
        outs = []
        for n, h in enumerate(range(0, H_R, 2)):
            c1 = slice(h * DK_R, (h + 1) * DK_R)
            c2 = slice((h + 1) * DK_R, (h + 2) * DK_R)
            a = jnp.where(keep2[d], both[n][:, 2 * DV_R:2 * DV_R + 2 * Lf], 0.0).astype(BF16)
            outs.append(both[n][:, 0:2 * DV_R] + _dot(a, side_by_side(vb[:, c1], vb[:, c2], zero_r)))
        out_ref[rows, :] = jnp.concatenate(outs, axis=1)

    def fast_chunks(s, first):
        gens = []
        for i in range(unroll):
            gens.append(fast_direction(0, s, first + i, of_ref))
            gens.append(fast_direction(1, s, n_fast - 1 - (first + i), ob_ref))
        _in_turn(gens)

    def finish_rows(rows):
        o = of_ref[rows, :] + ob_ref[rows, :]
        g = proj_ref[rows, g_cols]
        parts = [_rms(o[:, h * DV_R:(h + 1) * DV_R]) for h in range(H_R)]
        y = jnp.concatenate(parts, axis=1) * gn_ref[...]
        o_ref[rows, :] = y * (g / (1.0 + jnp.exp(-g)))

    def finish_all():
        def body(c, carry):
            finish_rows(pl.ds(pl.multiple_of(c * L, L), L))
            return carry
        lax.fori_loop(0, nseq * n_chunks, body, 0)

    n_groups = nseq * seq // PROJ_ROWS
    inline = n_fast == unroll
    worst = jnp.zeros((1, W_R), F32)
    project(0)
    for g in range(n_groups):
        if g + 1 < n_groups:
            project(g + 1)
        worst = prepare(g, worst)
        if inline and ((g + 1) * PROJ_ROWS) % seq == 0:
            s_done = (g + 1) * PROJ_ROWS // seq - 1
            fast_chunks(s_done, 0)
            for c in range(n_chunks):
                finish_rows(slice(s_done * seq + c * L, s_done * seq + (c + 1) * L))
    if not inline:
        for s in range(nseq):
            def fast_step(c, carry, s=s):
                fast_chunks(s, c * unroll)
                return carry
            lax.fori_loop(0, n_fast // unroll, fast_step, 0)

    def direction(d, s, c, out_ref):
        rows = pl.ds(pl.multiple_of(s * seq + c * L, L), L)
        row = lax.broadcasted_iota(jnp.int32, (L, W_R), 0)
        code = code_ref[d]
        tri = jnp.where(code >= 0, 1.0, 0.0).astype(BF16)
        q = proj_ref[rows, q_cols]
        v = proj_ref[rows, i_cols]
        lf, k = gates(proj_ref[rows, f_cols[d]], d)
        lf = lf * LOG2E
        b = cumulative(lf, tri)
        yield

        vb = v.astype(BF16)
        a_h = [None] * H_R
        for j in range(1, n_levels + 1):
            half = 1 << (j - 1)
            blk = 2 * half
            u = row & (blk - 1)
            qside = (u >= half) if d == 0 else (u < half)
            if j == 1:
                e_j = jnp.where(qside, lf, 0.0)
            elif j == 2:
                up = pltpu.roll(lf, L - 1, axis=0)
                dn = pltpu.roll(lf, 1, axis=0)
                if d == 0:
                    e_j = jnp.where(u == 0, up, jnp.where(u == 1, 0.0, jnp.where(u == 2, lf, lf + dn)))
                else:
                    e_j = jnp.where(u == 0, lf + up, jnp.where(u == 1, lf, jnp.where(u == 2, 0.0, dn)))
            else:
                diff = b - _bcast_block_row(b, blk, half - 1 + d)
                e_j = jnp.where(qside, diff, -diff)
            z = (jnp.where(qside, q, k) * jnp.exp2(e_j)).astype(BF16)
            for h in range(H_R):
                cs = slice(h * DK_R, (h + 1) * DK_R)
                p_j = _dot_nt(z[:, cs], z[:, cs])
                a_h[h] = jnp.where(code == j, p_j, 0.0 if a_h[h] is None else a_h[h])
            yield

        edge = b[L - 1:L, :] if d == 0 else b[0:1, :]
        q_in = (q * jnp.exp2(b)).astype(BF16)
        k_out = (k * jnp.exp2(edge - b)).astype(BF16)
        dec = jnp.exp2(edge)
        qk = q * k
        outs = []
        for h in range(H_R):
            cs = slice(h * DK_R, (h + 1) * DK_R)
            st = st_ref[s, d, h]
            o = _dot(a_h[h].astype(BF16), vb[:, cs]) + _dot_nt(q_in[:, cs], st.astype(BF16))
            o = o + jnp.sum(qk[:, cs], axis=-1, keepdims=True) * v[:, cs]
            st_ref[s, d, h] = st * dec[:, cs] + _dot(v[:, cs].T.astype(BF16), k_out[:, cs])
            outs.append(o)
        out_ref[rows, :] = jnp.concatenate(outs, axis=1)

    def step(i, carry):
        s, c = i // n_chunks, i % n_chunks
        _in_turn([direction(0, s, c, of_ref), direction(1, s, n_chunks - 1 - c, ob_ref)])
        return carry

    @pl.when(jnp.max(worst) > FAST_LIMIT)
    def _():
        init_states()
        lax.fori_loop(0, nseq * n_chunks, step, 0)
        if inline:
            finish_all()

    if not inline:
        finish_all()

    if want_s:
        for s in range(nseq):
            own = s_ref.at[s] if has_prev else s_ref.at[s, layer]
            for d in range(2):
                for h in range(H_R):
                    own[d, h] = st_ref[s, d, h].T
            if not has_prev:
                for other in range(DEPTH):
                    if other != layer:
                        s_ref[s, other] = jnp.zeros((2, H_R, DK_R, DV_R), F32)


def _hgrn(hb, w_in, lb_raw, gn, layer, batch, seq, s0=None, s_prev=None, want_s=False):
    has_s0 = s0 is not None
    nseq = max(1, HGRN_ROWS // seq)
    rows = nseq * seq
    state_spec = pl.BlockSpec((nseq, None, 2, H_R, DK_R, DV_R), lambda b: (b, layer, 0, 0, 0, 0))
    in_specs = [pl.BlockSpec((rows, D_MODEL), lambda b: (b, 0)),
                pl.BlockSpec((D_MODEL, IN_R), lambda b: (0, 0)),
                pl.BlockSpec((DEPTH, 2, W_R), lambda b: (0, 0, 0)),
                _layer_spec(gn, layer)]
    args = [hb, w_in, lb_raw, gn]
    if has_s0:
        in_specs.append(state_spec)
        args.append(s0)
    out_shape = [jax.ShapeDtypeStruct((batch * seq, W_R), F32)]
    out_specs = [pl.BlockSpec((rows, W_R), lambda b: (b, 0))]
    aliases = {}
    if want_s:
        out_shape.append(jax.ShapeDtypeStruct((batch, DEPTH, 2, H_R, DK_R, DV_R), F32))
        if s_prev is not None:
            out_specs.append(state_spec)
            aliases[len(args)] = 1
            in_specs.append(pl.BlockSpec(memory_space=pl.ANY))
            args.append(s_prev)
        else:
            out_specs.append(pl.BlockSpec((nseq, DEPTH, 2, H_R, DK_R, DV_R), lambda b: (b, 0, 0, 0, 0, 0)))
    res = pl.pallas_call(
        functools.partial(_hgrn_kernel, layer=layer, seq=seq, nseq=nseq, has_s0=has_s0,
                          has_prev=bool(aliases), want_s=want_s),
        out_shape=out_shape,
        grid=(batch // nseq,),
        in_specs=in_specs,
        out_specs=out_specs,
        scratch_shapes=[pltpu.VMEM((rows, IN_R), F32),
                        pltpu.VMEM((rows, W_R), F32), pltpu.VMEM((rows, W_R), F32),
                        pltpu.VMEM((nseq, 2, H_R, DV_R, DK_R), F32),
                        pltpu.VMEM((2, HGRN_CHUNK, HGRN_CHUNK), jnp.int32),
                        pltpu.VMEM((2, rows, W_R), F32), pltpu.VMEM((2, rows, W_R), F32)],
        input_output_aliases=aliases,
        compiler_params=_params("arbitrary"),
        name="hgrn",
    )(*args)
    return res if want_s else (res[0], None)


def _head_mean_square(x):
    shift = HEAD_DIM.bit_length() - 1
    r = lax.broadcasted_iota(jnp.int32, (LANES, LANES), 0) >> shift
    c = lax.broadcasted_iota(jnp.int32, (LANES, LANES), 1) >> shift
    ones = jnp.where(r == c, 1.0, 0.0).astype(BF16)
    sq = (x * x).astype(BF16)
    parts = [_dot(sq[:, n * LANES:(n + 1) * LANES], ones) for n in range(x.shape[1] // LANES)]
    return (parts[0] if len(parts) == 1 else jnp.concatenate(parts, axis=1)) * (1.0 / HEAD_DIM)


def _attn_kernel(*refs, seq, nseq, past, n_prev, layer):
    refs = list(refs)
    h_ref, wq0_ref, wq1_ref, wkv_ref, qg_ref, kg_ref = refs[:6]
    pos = 6
    if past:
        ck_ref, cv_ref, cos_ref, sa_ref, sb_ref = refs[pos:pos + 5]
        pos += 5
        o_ref = refs[pos]
        pos += 1
    else:
        pos += n_prev
        o_ref, kn_ref, vn_ref = refs[pos:pos + 3]
        pos += 3
    qs_ref, ks_ref, vt_ref = refs[pos:pos + 3]

    def rope(x):
        up = pltpu.roll(x, LANES - ROPE_AXIS_DIM // 2, axis=1)
        dn = pltpu.roll(x, ROPE_AXIS_DIM // 2, axis=1)
        return x * cos_ref[...] + up * sa_ref[...] + dn * sb_ref[...]

    hb = h_ref[...]
    q = jnp.concatenate([_dot(hb, wq0_ref[...]), _dot(hb, wq1_ref[...])], axis=1)
    kv = _dot(hb, wkv_ref[...])
    k = kv[:, 0:LANES]
    v = kv[:, LANES:2 * LANES]
    qn = q * lax.rsqrt(_head_mean_square(q) + EPS) * qg_ref[...]
    kn = k * lax.rsqrt(_head_mean_square(k) + EPS) * kg_ref[...]
    scale = HEAD_DIM ** -0.5 * LOG2E
    low = lax.broadcasted_iota(jnp.int32, (nseq * seq, LANES), 1) < HEAD_DIM
    pairs_per_kv = (W_A // LANES) // KV_A
    q_top2 = HEAD_DIM * scale * scale * jnp.max(qg_ref[...] * qg_ref[...])
    k_new_top2 = HEAD_DIM * jnp.max(kg_ref[...] * kg_ref[...])
    for p in range(W_A // LANES):
        cs = slice(p * LANES, (p + 1) * LANES)
        qp = (rope(qn[:, cs]) if past else qn[:, cs]) * scale
        even = jnp.where(low, qp, 0.0)
        odd = jnp.where(low, 0.0, qp)
        if p // pairs_per_kv == 0:
            odd = pltpu.roll(odd, HEAD_DIM, axis=1)
        else:
            even = pltpu.roll(even, HEAD_DIM, axis=1)
        qs_ref[0, :, cs] = even.astype(BF16)
        qs_ref[1, :, cs] = odd.astype(BF16)
    k_top2 = k_new_top2
    if past:
        k_top2 = jnp.maximum(k_top2, jnp.max(_head_mean_square(ck_ref[...])) * HEAD_DIM)
        ks_ref[0] = jnp.concatenate([ck_ref[...], rope(kn)], axis=0).astype(BF16)
        vt_ref[0] = jnp.concatenate([cv_ref[...], v], axis=0).T.astype(BF16)
    else:
        for i in range(nseq):
            rs = slice(i * seq, (i + 1) * seq)
            ks_ref[i] = kn[rs, :].astype(BF16)
            vt_ref[i] = v[rs, :].T.astype(BF16)
            if n_prev:
                kn_ref[i] = kn[rs, :]
                vn_ref[i] = v[rs, :]
            else:
                for other in range(DEPTH):
                    kn_ref[i, other] = kn[rs, :] if other == layer else jnp.zeros((seq, LANES), F32)
                    vn_ref[i, other] = v[rs, :] if other == layer else jnp.zeros((seq, LANES), F32)

    tq = min(ATTN_QROWS, seq)
    total = past + seq

    kb = ATTN_KEYS if total % ATTN_KEYS == 0 else total
    n_blocks = total // kb
    n_tiles = seq // tq
    inline = not past
    group = 1 if inline else ATTN_TILES
    units = ([(i, r) for i in range(nseq) for r in range(n_tiles)] if inline
             else [(0, -1 - j) for j in range(group)])
    steps = [(u, p, kc) for u in units for p in range(W_A // LANES) for kc in range(n_blocks)]

    worst = jnp.sqrt(q_top2 * k_top2) * BOUND_SLACK

    def tile(r, carry, bounded):
        def q_rows(unit):
            i, static_r = unit
            start = i * seq + (r * group - 1 - static_r if static_r < 0 else static_r) * tq
            return slice(start, start + tq) if isinstance(start, int) else pl.ds(pl.multiple_of(start, tq), tq)

        def q_pair(unit, p):
            cs = slice(p * LANES, (p + 1) * LANES)
            return jnp.concatenate([qs_ref[0, q_rows(unit), cs], qs_ref[1, q_rows(unit), cs]], axis=0)

        def scores(step):
            unit, p, kc = step
            return _dot_nt(ks_ref[unit[0], kc * kb:(kc + 1) * kb, :], q_pair(unit, p))

        ahead = [scores(st) for st in steps[:ATTN_LOOKAHEAD]]
        for n, (unit, p, kc) in enumerate(steps):
            s = ahead.pop(0)
            if n + ATTN_LOOKAHEAD < len(steps):
                ahead.append(scores(steps[n + ATTN_LOOKAHEAD]))
            kh = p // pairs_per_kv
            v_t = vt_ref[unit[0], kh * HEAD_DIM:(kh + 1) * HEAD_DIM, kc * kb:(kc + 1) * kb]
            if bounded:
                pexp = jnp.exp2(s - worst)
                part = jnp.sum(pexp, axis=0, keepdims=True)
                prod = _dot(v_t, pexp.astype(BF16))
                den = part if kc == 0 else den + part
                o2 = prod if kc == 0 else o2 + prod
            else:
                s_max = jnp.max(s, axis=0, keepdims=True)
                if kc == 0:
                    m = s_max
                    pexp = jnp.exp2(s - m)
                    den = jnp.sum(pexp, axis=0, keepdims=True)
                    o2 = _dot(v_t, pexp.astype(BF16))
                else:
                    m_new = jnp.maximum(m, s_max)
                    alpha = jnp.exp2(m - m_new)
                    pexp = jnp.exp2(s - m_new)
                    den = den * alpha + jnp.sum(pexp, axis=0, keepdims=True)
                    o2 = o2 * alpha + _dot(v_t, pexp.astype(BF16))
                    m = m_new
            if kc == n_blocks - 1:
                o2 = o2 / den
                o_ref[q_rows(unit), p * LANES:(p + 1) * LANES] = jnp.concatenate([o2[:, 0:tq], o2[:, tq:2 * tq]], axis=0).T
        return carry

    def tiles(bounded):
        if inline:
            tile(None, 0, bounded)
        else:
            assert nseq == 1
            lax.fori_loop(0, n_tiles // group, functools.partial(tile, bounded=bounded), 0)

    tiles(True)

    @pl.when(worst > BOUND_LIMIT)
    def _():
        tiles(False)


def _attn(hb, w_in, qg, kg, batch, seq, layer, cache=None, rope_tabs=None, kv_prev=None):
    past = 0 if cache is None else cache[0].shape[2]
    wide = 2 * LANES
    w_specs = [pl.BlockSpec((D_MODEL, wide), lambda b, j=j: (0, IN_R // wide + j))
               for j in range(IN_A // wide)]
    nseq = max(1, ATTN_ROWS // seq)
    rows = nseq * seq
    in_specs = [pl.BlockSpec((rows, D_MODEL), lambda b: (b, 0)), *w_specs,
                _layer_spec(qg, layer), _layer_spec(kg, layer)]
    args = [hb, w_in, w_in, w_in, qg, kg]
    out_shape = [jax.ShapeDtypeStruct((batch * seq, W_A), F32)]
    out_specs = [pl.BlockSpec((rows, W_A), lambda b: (b, 0))]
    aliases = {}
    if past:
        ck, cv = cache
        cspec = pl.BlockSpec((None, None, past, LANES), lambda b: (b, layer, 0, 0))
        tspec = pl.BlockSpec((seq, LANES), lambda b: (0, 0))
        in_specs += [cspec, cspec, tspec, tspec, tspec]
        args += [ck, cv, *rope_tabs]
    else:
        for i in range(2):
            out_shape.append(jax.ShapeDtypeStruct((batch, DEPTH, seq, LANES), F32))
            if kv_prev is not None:
                out_specs.append(pl.BlockSpec((nseq, None, seq, LANES), lambda b: (b, layer, 0, 0)))
                aliases[len(args)] = 1 + i
                in_specs.append(pl.BlockSpec(memory_space=pl.ANY))
                args.append(kv_prev[i])
            else:
                out_specs.append(pl.BlockSpec((nseq, DEPTH, seq, LANES), lambda b: (b, 0, 0, 0)))
    res = pl.pallas_call(
        functools.partial(_attn_kernel, seq=seq, nseq=nseq, past=past, n_prev=len(aliases), layer=layer),
        out_shape=out_shape,
        grid=(batch // nseq,),
        in_specs=in_specs,
        out_specs=out_specs,
        scratch_shapes=[pltpu.VMEM((2, rows, W_A), BF16),
                        pltpu.VMEM((nseq, past + seq, LANES), BF16),
                        pltpu.VMEM((nseq, LANES, past + seq), BF16)],
        input_output_aliases=aliases,
        compiler_params=_params("arbitrary"),
        name="attn",
    )(*args)
    return res


def _outffn_kernel(*refs, final, n_cast):
    refs = list(refs)
    x_ref, or_ref, oa_ref, mod_ref, g2_ref, wout_ref, w1_ref, w2_ref, last_ref = refs[:9]
    pos = 9
    if not final:
        nmod_ref = refs[pos]
        pos += 1
    n_out = 1 if final else 2
    for src, dst in zip(refs[pos:pos + n_cast], refs[pos + n_cast + n_out:pos + 2 * n_cast + n_out]):
        dst[...] = src[...].astype(BF16)
    pos += n_cast
    o_ref = refs[pos]
    gate1 = mod_ref[:, 2 * D_MODEL:3 * D_MODEL]
    shift2 = mod_ref[:, 3 * D_MODEL:4 * D_MODEL]
    scale2 = mod_ref[:, 4 * D_MODEL:5 * D_MODEL]
    gate2 = mod_ref[:, 5 * D_MODEL:6 * D_MODEL]
    hid_ref = refs[-1]

    def rows_part(rs):
        mix = _dot(or_ref[rs, :].astype(BF16), wout_ref[0:W_R, :]) + _dot(oa_ref[rs, :].astype(BF16), wout_ref[W_R:W_R + W_A, :])
        yield
        x1 = x_ref[rs, :] + gate1 * mix
        h2 = _prenorm(x1, shift2, scale2, g2_ref[...])
        for j in range(D_FF // FFN_CHUNK):
            cs = slice(j * FFN_CHUNK, (j + 1) * FFN_CHUNK)
            hid = jnp.maximum(_dot(h2, w1_ref[:, cs]), 0.0)
            hid_ref[rs, cs] = (hid * hid).astype(BF16)
        yield
        x2 = x1 + gate2 * _dot(hid_ref[rs, :], w2_ref[...])
        yield
        if final:
            o_ref[rs, :] = _rms(x2) * last_ref[...]
        else:
            o_ref[rs, :] = x2
            hn_ref = refs[pos + 1]
            hn_ref[rs, :] = _prenorm(x2, nmod_ref[:, 0:D_MODEL], nmod_ref[:, D_MODEL:2 * D_MODEL], last_ref[...])

    n_rows = x_ref.shape[0]
    part = n_rows // FFN_PARTS
    _in_turn([rows_part(slice(i * part, (i + 1) * part)) for i in range(FFN_PARTS)])


def _outffn(x, o_r, o_a, mod, g1, g2, wout, w1, w2, gf, layer, rows_per_seq, ctx, cast=()):
    n = x.shape[0]
    tm = ROW_TILE
    steps = n // tm
    final = layer == DEPTH - 1
    row = lambda w: pl.BlockSpec((tm, w), lambda i: (i, 0))
    whole = lambda a: pl.BlockSpec(a.shape, lambda i: (0, 0))
    in_specs = [row(D_MODEL), row(W_R), row(W_A),
                _mod_spec(layer, rows_per_seq, tm, ctx),
                _layer_spec(g2, layer), whole(wout), whole(w1), whole(w2)]
    args = [x, o_r, o_a, mod, g2, wout, w1, w2]
    out_shape = [jax.ShapeDtypeStruct((n, D_MODEL), F32)]
    out_specs = [row(D_MODEL)]
    if final:
        in_specs.append(pl.BlockSpec(gf.shape, lambda i: (0, 0)))
        args.append(gf)
    else:
        in_specs += [_layer_spec(g1, layer + 1), _mod_spec(layer + 1, rows_per_seq, tm, ctx)]
        args += [g1, mod]
        out_shape.append(jax.ShapeDtypeStruct((n, D_MODEL), BF16))
        out_specs.append(row(D_MODEL))
    for a in cast:
        r, c = a.shape[1:]
        in_specs.append(pl.BlockSpec((None, r // steps, c), lambda i: (layer + 1, i, 0)))
        args.append(a)
        out_shape.append(jax.ShapeDtypeStruct((r, c), BF16))
        out_specs.append(pl.BlockSpec((r // steps, c), lambda i: (i, 0)))
    res = pl.pallas_call(
        functools.partial(_outffn_kernel, final=final, n_cast=len(cast)),
        out_shape=out_shape,
        grid=(steps,),
        in_specs=in_specs,
        out_specs=out_specs,
        scratch_shapes=[pltpu.VMEM((tm, D_FF), BF16)],
        compiler_params=_params("arbitrary"),
        name="outffn",
    )(*args)
    if final:
        return (res[0], None)
    return tuple(res)


def _rope_tables(seq):
    rows = seq // GRID_W
    rowi = jnp.repeat(jnp.arange(rows, dtype=F32), GRID_W)
    coli = jnp.tile(jnp.arange(GRID_W, dtype=F32), rows)
    inv = ROPE_THETA ** (-jnp.arange(0, ROPE_AXIS_DIM, 2, dtype=F32) / ROPE_AXIS_DIM)
    ar = rowi[:, None] * inv[None, :]
    ac = coli[:, None] * inv[None, :]
    ang = jnp.concatenate([ar, ar, ac, ac], axis=-1)
    cos = jnp.tile(jnp.cos(ang), (1, LANES // HEAD_DIM))
    sin = jnp.tile(jnp.sin(ang), (1, LANES // HEAD_DIM))
    first = (jnp.arange(LANES) % ROPE_AXIS_DIM) < ROPE_AXIS_DIM // 2
    return cos, jnp.where(first, -sin, 0.0), jnp.where(first, 0.0, sin)


def kernel(x_prompt, x_sample, cache_k, cache_v, state_hgrn, c, c_ctx, w_mod, b_mod, norm1_g, w_in, lb_raw,
           hgrn_norm_g, q_norm_g, k_norm_g, w_out, norm2_g, w1, w2, final_norm_g):
    bp, sp, _ = x_prompt.shape
    bs, ss, _ = x_sample.shape
    past = cache_k.shape[2]

    cvec = jnp.zeros((MOD_ROWS, D_MODEL), F32).at[0:bs].set(c).at[MOD_ROWS // 2].set(c_ctx)
    mod = _modulation(cvec, w_mod, b_mod).reshape(DEPTH, MOD_ROWS, 1, MOD_W)

    stacked = (w_in, w_out, w1, w2)
    w_in_b, w_out_b, w1_b, w2_b = (w[0].astype(BF16) for w in stacked)
    ck =cache_k.reshape(bs, DEPTH, past, KV_A * HEAD_DIM)
    cv = cache_v.reshape(bs, DEPTH, past, KV_A * HEAD_DIM)
    tabs = _rope_tables(ss)
    gf = final_norm_g.reshape(1, D_MODEL)
    g1 = norm1_g.reshape(DEPTH, 1, D_MODEL)
    g2 = norm2_g.reshape(DEPTH, 1, D_MODEL)
    gn = jnp.tile(hgrn_norm_g, (1, H_R)).reshape(DEPTH, 1, W_R)
    qg = jnp.tile(q_norm_g, (1, H_A)).reshape(DEPTH, 1, W_A)
    kg = jnp.tile(k_norm_g, (1, KV_A)).reshape(DEPTH, 1, LANES)

    xp = x_prompt.reshape(bp * sp, D_MODEL)
    xs = x_sample.reshape(bs * ss, D_MODEL)
    hp = _first_prenorm(xp, mod, g1, sp, True)
    hs = _first_prenorm(xs, mod, g1, ss, False)
    new_s, new_kv = None, None
    for l in range(DEPTH):
        o_r, new_s = _hgrn(hp, w_in_b, lb_raw, gn, l, bp, sp, s_prev=new_s, want_s=True)
        o_a, *new_kv = _attn(hp, w_in_b, qg, kg, bp, sp, l, kv_prev=new_kv)
        xp, hp, *next_w = _outffn(xp, o_r, o_a, mod, g1, g2, w_out_b, w1_b, w2_b, gf, l, sp, True,
                                  cast=stacked if l + 1 < DEPTH else ())

        o_r, _ = _hgrn(hs, w_in_b, lb_raw, gn, l, bs, ss, s0=state_hgrn)
        o_a, = _attn(hs, w_in_b, qg, kg, bs, ss, l, cache=(ck, cv), rope_tabs=tabs)
        xs, hs = _outffn(xs, o_r, o_a, mod, g1, g2, w_out_b, w1_b, w2_b, gf, l, ss, False)
        if next_w:
            w_in_b, w_out_b, w1_b, w2_b = next_w

    y_prompt = xp.reshape(bp, sp, D_MODEL)
    y_sample = xs.reshape(bs, ss, D_MODEL)
    new_k = new_kv[0].reshape(bp, DEPTH, sp, KV_A, HEAD_DIM)
    new_v = new_kv[1].reshape(bp, DEPTH, sp, KV_A, HEAD_DIM)
    return (y_prompt, y_sample, new_k, new_v, new_s)
```
